```python
import math
import jax
import jax.numpy as jnp
from jax import lax
import numpy as np

D_MODEL = 4096
BATCH = 4
SEQ = 2048
DEPTH = 2
DEC_BATCH = 8
DEC_SEQ = 1
PAST_LEN = 16384
PAGE_SIZE = 128

N_A = DEPTH // 2
N_B = DEPTH - N_A
GLA_HEADS = 4
GLA_DK = D_MODEL // (2 * GLA_HEADS)
GLA_DV = D_MODEL // GLA_HEADS
GLA_GATE_RANK = 16
GLA_GATE_TEMP = 16.0
GLA_CHUNK = 64
ATT_HEADS = 32
ATT_HD = D_MODEL // ATT_HEADS
MOBA_BLOCK = 256
MOBA_TOPK = 3
MOBA_QSUB = 16
ROPE_THETA = 10000.0
FFN_DIM = ((8 * D_MODEL // 3) + 255) // 256 * 256
CONV_W = 3
EPS = 1e-6

kernel_name = "yoco_gla_moba_convffn_step"

F32 = jnp.float32


def rmsnorm(x, g):
    xf = x.astype(F32)
    y = xf * lax.rsqrt(jnp.mean(xf * xf, axis=-1, keepdims=True) + EPS)
    return (y * g.astype(F32)).astype(x.dtype)


def rope(x, pos):
    half = x.shape[-1] // 2
    inv = ROPE_THETA ** (-jnp.arange(half, dtype=F32) / half)
    ang = pos[:, None] * inv[None, :]
    cos = jnp.cos(ang)[None, :, None, :]
    sin = jnp.sin(ang)[None, :, None, :]
    xf = x.astype(F32)
    x1, x2 = xf[..., :half], xf[..., half:]
    return jnp.concatenate([x1 * cos - x2 * sin, x2 * cos + x1 * sin], axis=-1).astype(x.dtype)


def gla_scan(q, k, v, logg, s0):
    B, T, H, DK = q.shape
    DV = v.shape[-1]
    C = min(GLA_CHUNK, T)
    pad = (-T) % C
    N = (T + pad) // C

    def blocks(a):
        a = jnp.pad(a, ((0, 0), (0, pad), (0, 0), (0, 0)))
        return a.reshape(B, N, C, H, a.shape[-1]).transpose(1, 0, 3, 2, 4)

    qc, kc, vc, gc = blocks(q), blocks(k), blocks(v), blocks(logg)
    causal = jnp.tril(jnp.ones((C, C), dtype=bool))

    def step(s, inp):
        qb, kb, vb, gb = inp
        b = jnp.cumsum(gb, axis=2)
        b_last = b[:, :, -1:, :]
        qe = qb * jnp.exp(b)
        att = jnp.where(causal, jnp.einsum("bhcd,bhed->bhce", qe, kb * jnp.exp(-b)), 0.0)
        o = jnp.einsum("bhce,bhev->bhcv", att, vb) + jnp.einsum("bhcd,bhdv->bhcv", qe, s)
        s_new = jnp.exp(b_last[:, :, 0, :])[..., None] * s + jnp.einsum("bhcd,bhcv->bhdv", kb * jnp.exp(b_last - b), vb)
        return s_new, o

    s_fin, o = lax.scan(step, s0, (qc, kc, vc, gc))
    o = o.transpose(1, 0, 3, 2, 4).reshape(B, N * C, H, DV)[:, :T]
    return o, s_fin


def gla_mixer(xn, s0, w_in, w_a2, b_a2, g_head, w_out):
    B, T, _ = xn.shape
    dk, dv = GLA_HEADS * GLA_DK, GLA_HEADS * GLA_DV
    proj = xn @ w_in
    q, k, v, r, a = jnp.split(proj, [dk, 2 * dk, 2 * dk + dv, 2 * dk + 2 * dv], axis=-1)
    logg = jax.nn.log_sigmoid((a @ w_a2 + b_a2).astype(F32)) / GLA_GATE_TEMP
    o, s = gla_scan(
        q.astype(F32).reshape(B, T, GLA_HEADS, GLA_DK) * (GLA_DK ** -0.5),
        k.astype(F32).reshape(B, T, GLA_HEADS, GLA_DK),
        v.astype(F32).reshape(B, T, GLA_HEADS, GLA_DV),
        logg.reshape(B, T, GLA_HEADS, GLA_DK),
        s0.astype(F32),
    )
    o = rmsnorm(o, g_head).reshape(B, T, dv)
    o = (o * jax.nn.silu(r.astype(F32))).astype(xn.dtype)
    return o @ w_out, s.astype(s0.dtype)


def moba_prompt(q, k, v):
    B, S, H, hd = q.shape
    scale = hd ** -0.5
    NB = -(-S // MOBA_BLOCK)
    pad = NB * MOBA_BLOCK - S
    kp = jnp.pad(k, ((0, 0), (0, pad), (0, 0), (0, 0)))
    vp = jnp.pad(v, ((0, 0), (0, pad), (0, 0), (0, 0)))
    kb = kp.reshape(B, NB, MOBA_BLOCK, H, hd)
    vb = vp.reshape(B, NB, MOBA_BLOCK, H, hd)
    qblk = jnp.arange(S) // MOBA_BLOCK
    ksel = min(MOBA_TOPK, NB - 1)
    if ksel > 0:
        means = jnp.mean(kb.astype(F32), axis=2)
        gate = jnp.einsum("bshd,bnhd->bshn", q.astype(F32), means)
        past = jnp.arange(NB)[None, :] < qblk[:, None]
        gate = jnp.where(past[None, :, None, :], gate, -jnp.inf)
        _, idx = lax.top_k(gate, ksel)
        valid = idx < qblk[None, :, None, None]
    bi = jnp.arange(B)[:, None, None, None]
    hi = jnp.arange(H)[None, None, :, None]

    def attend_block(i):
        s0 = i * MOBA_QSUB
        qc = lax.dynamic_slice_in_dim(q, s0, MOBA_QSUB, axis=1).astype(F32) * scale
        c0 = (s0 // MOBA_BLOCK) * MOBA_BLOCK
        ko = lax.dynamic_slice_in_dim(kp, c0, MOBA_BLOCK, axis=1).astype(F32)
        vo = lax.dynamic_slice_in_dim(vp, c0, MOBA_BLOCK, axis=1).astype(F32)
        qpos = s0 + jnp.arange(MOBA_QSUB)
        kpos = c0 + jnp.arange(MOBA_BLOCK)
        s_own = jnp.einsum("bqhd,bkhd->bqhk", qc, ko)
        s_own = jnp.where((kpos[None, :] <= qpos[:, None])[None, :, None, :], s_own, -jnp.inf)
        if ksel > 0:
            ic = lax.dynamic_slice_in_dim(idx, s0, MOBA_QSUB, axis=1)
            vm = lax.dynamic_slice_in_dim(valid, s0, MOBA_QSUB, axis=1)
            kg = kb[bi, ic, :, hi].astype(F32)
            vg = vb[bi, ic, :, hi].astype(F32)
            s_past = jnp.einsum("bqhd,bqhkld->bqhkl", qc, kg)
            s_past = jnp.where(vm[..., None], s_past, -jnp.inf).reshape(B, MOBA_QSUB, H, ksel * MOBA_BLOCK)
            p = jax.nn.softmax(jnp.concatenate([s_past, s_own], axis=-1), axis=-1)
            pp = p[..., : ksel * MOBA_BLOCK].reshape(B, MOBA_QSUB, H, ksel, MOBA_BLOCK)
            o = jnp.einsum("bqhkl,bqhkld->bqhd", pp, vg) + jnp.einsum("bqhk,bkhd->bqhd", p[..., ksel * MOBA_BLOCK:], vo)
        else:
            o = jnp.einsum("bqhk,bkhd->bqhd", jax.nn.softmax(s_own, axis=-1), vo)
        return o.astype(q.dtype)

    out = lax.map(attend_block, jnp.arange(S // MOBA_QSUB))
    return out.transpose(1, 0, 2, 3, 4).reshape(B, S, H, hd)


def moba_sample(q, k_new, v_new, cache_k, cache_v, page_table):
    DB, T, H, hd = q.shape
    scale = hd ** -0.5
    n_pages = page_table.shape[1]
    ppb = MOBA_BLOCK // PAGE_SIZE
    nbp = n_pages // ppb
    ksel = min(MOBA_TOPK, nbp)
    qf = q.astype(F32) * scale
    tail = page_table[:, nbp * ppb:]
    rem = tail.shape[1] * PAGE_SIZE
    ko = jnp.concatenate([cache_k[tail].reshape(DB, rem, H, hd), k_new], axis=1).astype(F32)
    vo = jnp.concatenate([cache_v[tail].reshape(DB, rem, H, hd), v_new], axis=1).astype(F32)
    allowed = jnp.arange(rem + T)[None, :] <= rem + jnp.arange(T)[:, None]
    s_own = jnp.where(allowed[None, :, None, :], jnp.einsum("bthd,bkhd->bthk", qf, ko), -jnp.inf)
    if ksel > 0:
        blk_pages = page_table[:, : nbp * ppb].reshape(DB, nbp, ppb)
        means = lax.map(lambda pt: jnp.mean(cache_k[pt].astype(F32), axis=(1, 2)), blk_pages)
        gate = jnp.einsum("bthd,bnhd->bthn", q.astype(F32), means)
        _, idx = lax.top_k(gate, ksel)
        pages = blk_pages[jnp.arange(DB)[:, None, None, None], idx]
        hi = jnp.arange(H)[None, None, :, None, None]
        L = ksel * MOBA_BLOCK
        kg = cache_k[pages, :, hi].reshape(DB, T, H, L, hd).astype(F32)
        vg = cache_v[pages, :, hi].reshape(DB, T, H, L, hd).astype(F32)
        s_past = jnp.einsum("bthd,bthkd->bthk", qf, kg)
        p = jax.nn.softmax(jnp.concatenate([s_past, s_own], axis=-1), axis=-1)
        o = jnp.einsum("bthk,bthkd->bthd", p[..., :L], vg) + jnp.einsum("bthk,bkhd->bthd", p[..., L:], vo)
    else:
        o = jnp.einsum("bthk,bkhd->bthd", jax.nn.softmax(s_own, axis=-1), vo)
    return o.astype(q.dtype)


def conv_ffn(xn, buf, w_up, w_conv, b_conv, w_down):
    T = xn.shape[1]
    u = xn @ w_up
    ext = jnp.concatenate([buf.astype(u.dtype), u], axis=1)
    c = b_conv + w_conv[0] * ext[:, 0:T]
    for i in range(1, CONV_W):
        c = c + w_conv[i] * ext[:, i:i + T]
    a, g = jnp.split(c, 2, axis=-1)
    return (jax.nn.silu(a) * g) @ w_down, ext[:, T:]


def trunk(x, pos, gla_init, conv_init, attend, g_mix, w_in_a, w_a2, b_a2, g_gla_head, w_out_a,
          g_kv, w_kv, w_q, w_out_b, g_ffn, w_up, w_conv, b_conv, w_down, g_final):
    B, T, _ = x.shape
    da = ATT_HEADS * ATT_HD
    h = x
    gla_states, conv_states = [], []
    k_sh = v_sh = None
    for l in range(DEPTH):
        hn = rmsnorm(h, g_mix[l])
        if l < N_A:
            y, s = gla_mixer(hn, gla_init[l], w_in_a[l], w_a2[l], b_a2[l], g_gla_head[l], w_out_a[l])
            gla_states.append(s)
        else:
            j = l - N_A
            if j == 0:
                kv = rmsnorm(h, g_kv) @ w_kv
                k_sh, v_sh = jnp.split(kv, 2, axis=-1)
                k_sh = rope(k_sh.reshape(B, T, ATT_HEADS, ATT_HD), pos)
                v_sh = v_sh.reshape(B, T, ATT_HEADS, ATT_HD)
            q = rope((hn @ w_q[j]).reshape(B, T, ATT_HEADS, ATT_HD), pos)
            y = attend(q, k_sh, v_sh).reshape(B, T, da) @ w_out_b[j]
        h = h + y
        f, buf = conv_ffn(rmsnorm(h, g_ffn[l]), conv_init[l], w_up[l], w_conv[l], b_conv[l], w_down[l])
        h = h + f
        conv_states.append(buf)
    return rmsnorm(h, g_final), k_sh, v_sh, jnp.stack(gla_states), jnp.stack(conv_states)


def setup_inputs(seed: int = 0) -> dict:
    key = jax.random.key(seed)
    ks = jax.random.split(key, 24)
    n_pages = PAST_LEN // PAGE_SIZE
    n_pool = (5 * DEC_BATCH * n_pages + 3) // 4
    dk, dv, da = GLA_HEADS * GLA_DK, GLA_HEADS * GLA_DV, ATT_HEADS * ATT_HD
    in_a = 2 * dk + 2 * dv + GLA_GATE_RANK

    def nrm(k, shape, fan_in):
        return jax.random.normal(k, shape, F32) * (fan_in ** -0.5)

    def gain(k, shape):
        return 1.0 + 0.05 * jax.random.normal(k, shape, F32)

    page_table = jax.random.permutation(ks[4], n_pool)[: DEC_BATCH * n_pages].reshape(DEC_BATCH, n_pages).astype(jnp.int32)
    return {
        "x_prompt": jax.random.normal(ks[0], (BATCH, SEQ, D_MODEL), F32),
        "x_sample": jax.random.normal(ks[1], (DEC_BATCH, DEC_SEQ, D_MODEL), F32),
        "cache_k": jax.random.normal(ks[2], (n_pool, PAGE_SIZE, ATT_HEADS, ATT_HD), F32),
        "cache_v": jax.random.normal(ks[3], (n_pool, PAGE_SIZE, ATT_HEADS, ATT_HD), F32),
        "page_table": page_table,
        "state_gla": jax.random.normal(ks[5], (N_A, DEC_BATCH, GLA_HEADS, GLA_DK, GLA_DV), F32),
        "state_conv": jax.random.normal(ks[6], (DEPTH, DEC_BATCH, CONV_W - 1, 2 * FFN_DIM), F32),
        "g_mix": gain(ks[7], (DEPTH, D_MODEL)),
        "w_in_a": nrm(ks[8], (N_A, D_MODEL, in_a), D_MODEL),
        "w_a2": nrm(ks[9], (N_A, GLA_GATE_RANK, dk), GLA_GATE_RANK),
        "b_a2": 0.1 * jax.random.normal(ks[10], (N_A, dk), F32),
        "g_gla_head": gain(ks[11], (N_A, GLA_DV)),
        "w_out_a": nrm(ks[12], (N_A, dv, D_MODEL), dv),
        "g_kv": gain(ks[13], (D_MODEL,)),
        "w_kv": nrm(ks[14], (D_MODEL, 2 * da), D_MODEL),
        "w_q": nrm(ks[15], (N_B, D_MODEL, da), D_MODEL),
        "w_out_b": nrm(ks[16], (N_B, da, D_MODEL), da),
        "g_ffn": gain(ks[17], (DEPTH, D_MODEL)),
        "w_up": nrm(ks[18], (DEPTH, D_MODEL, 2 * FFN_DIM), D_MODEL),
        "w_conv": nrm(ks[19], (DEPTH, CONV_W, 2 * FFN_DIM), CONV_W),
        "b_conv": 0.02 * jax.random.normal(ks[20], (DEPTH, 2 * FFN_DIM), F32),
        "w_down": nrm(ks[21], (DEPTH, FFN_DIM, D_MODEL), FFN_DIM),
        "g_final": gain(ks[22], (D_MODEL,)),
    }


def reference(x_prompt, x_sample, cache_k, cache_v, page_table, state_gla, state_conv,
              g_mix, w_in_a, w_a2, b_a2, g_gla_head, w_out_a, g_kv, w_kv, w_q, w_out_b,
              g_ffn, w_up, w_conv, b_conv, w_down, g_final):
    Bp, S, _ = x_prompt.shape
    Bs, T, _ = x_sample.shape
    gla0 = jnp.zeros((N_A, Bp, GLA_HEADS, GLA_DK, GLA_DV), x_prompt.dtype)
    conv0 = jnp.zeros((DEPTH, Bp, CONV_W - 1, 2 * FFN_DIM), x_prompt.dtype)
    y_prompt, k_prompt, v_prompt, gla_prompt, conv_prompt = trunk(
        x_prompt, jnp.arange(S, dtype=F32), gla0, conv0, moba_prompt,
        g_mix, w_in_a, w_a2, b_a2, g_gla_head, w_out_a, g_kv, w_kv, w_q, w_out_b,
        g_ffn, w_up, w_conv, b_conv, w_down, g_final)

    def attend_sample(q, k, v):
        return moba_sample(q, k, v, cache_k, cache_v, page_table)

    y_sample, k_sample, v_sample, gla_sample, conv_sample = trunk(
        x_sample, PAST_LEN + jnp.arange(T, dtype=F32), state_gla, state_conv, attend_sample,
        g_mix, w_in_a, w_a2, b_a2, g_gla_head, w_out_a, g_kv, w_kv, w_q, w_out_b,
        g_ffn, w_up, w_conv, b_conv, w_down, g_final)
    return (y_prompt, y_sample, k_prompt, v_prompt, gla_prompt, conv_prompt, k_sample, v_sample, gla_sample, conv_sample)
```

```python
import functools

import jax
import jax.numpy as jnp
from jax import lax
from jax.experimental import pallas as pl
from jax.experimental.pallas import tpu as pltpu

F32 = jnp.float32
BF16 = jnp.bfloat16
HIGHEST = lax.Precision.HIGHEST

GLA_GATE_TEMP = 16.0
GLA_CHUNK = 64
MOBA_BLOCK = 256
MOBA_TOPK = 3
ROPE_THETA = 10000.0
EPS = 1e-6

LANES = 128
SUBLANES = 8
VMEM_PHYSICAL_BYTES = 64 * 1024 * 1024
VMEM_BUDGET_BYTES = 44 * 1024 * 1024
VMEM_COMPILER_SLACK_BYTES = 8 * 1024 * 1024
NEG_BIG = -1e30


def _vmem_limit(planned_bytes):
    return int(min(planned_bytes + VMEM_COMPILER_SLACK_BYTES, VMEM_PHYSICAL_BYTES - 4 * 1024 * 1024))


def _params(planned_bytes, n_axes):
    return pltpu.CompilerParams(
        dimension_semantics=("arbitrary",) * n_axes,
        vmem_limit_bytes=_vmem_limit(planned_bytes),
    )


def _nt(a, b, precision=None):
    return lax.dot_general(a, b, (((1,), (1,)), ((), ())), precision=precision,
                           preferred_element_type=F32)


def _tn(a, b, precision=None):
    return lax.dot_general(a, b, (((0,), (0,)), ((), ())), precision=precision,
                           preferred_element_type=F32)


def _sigmoid(x):
    return 1.0 / (1.0 + jnp.exp(-x))


def _log_sigmoid(z):
    return jnp.minimum(z, 0.0) - jnp.log1p(jnp.exp(-jnp.abs(z)))


def _pick(n, candidates):
    for c in candidates:
        if n % c == 0:
            return c
    return n


def _norm_kernel(x_ref, g_ref, *o_refs):
    x = x_ref[...]
    y = x * lax.rsqrt(jnp.mean(x * x, axis=-1, keepdims=True) + EPS)
    for i, o_ref in enumerate(o_refs):
        o_ref[...] = (y * g_ref[i:i + 1, :]).astype(o_ref.dtype)


def _rmsnorm(x, gains, out_dtype):
    m, d = x.shape
    n_g = gains.shape[0]
    tm = _pick(m, (256, 128, 64, 32, 16, 8))
    planned = 2 * tm * d * 4 * (1 + n_g)
    outs = pl.pallas_call(
        _norm_kernel,
        grid=(m // tm,),
        in_specs=[pl.BlockSpec((tm, d), lambda i: (i, 0)),
                  pl.BlockSpec((n_g, d), lambda i: (0, 0))],
        out_specs=[pl.BlockSpec((tm, d), lambda i: (i, 0)) for _ in range(n_g)],
        out_shape=[jax.ShapeDtypeStruct((m, d), out_dtype) for _ in range(n_g)],
        compiler_params=_params(planned, 1),
        name="rmsnorm",
    )(x, gains)
    return outs


def _rope_tile(acc, cos, sin_signed):
    heads = []
    for c in range(acc.shape[1] // LANES):
        xc = acc[:, c * LANES:(c + 1) * LANES]
        heads.append(xc * cos + pltpu.roll(xc, LANES // 2, 1) * sin_signed)
    return heads[0] if len(heads) == 1 else jnp.concatenate(heads, axis=1)


def _cast_weight_tile(w_ref, wb_ref):
    k = w_ref.shape[0]
    ck = _pick(k, (512, 256, 128))

    def body(i, carry):
        r = pl.multiple_of(i * ck, ck)
        wb_ref[pl.ds(r, ck), :] = w_ref[pl.ds(r, ck), :].astype(BF16)
        return carry

    lax.fori_loop(0, k // ck, body, 0)


def _mm_kernel(*refs, has_res, has_rope):
    x_ref, w_ref = refs[0], refs[1]
    pos = 2
    res_ref = cos_ref = sin_ref = None
    if has_res:
        res_ref = refs[pos]
        pos += 1
    if has_rope:
        cos_ref, sin_ref = refs[pos], refs[pos + 1]
        pos += 2
    o_ref, wb_ref = refs[pos], refs[pos + 1]

    @pl.when(pl.program_id(1) == 0)
    def _():
        _cast_weight_tile(w_ref, wb_ref)

    acc = jnp.dot(x_ref[...], wb_ref[...], preferred_element_type=F32)
    if has_rope:
        acc = _rope_tile(acc, cos_ref[...], sin_ref[...])
    if has_res:
        acc = acc + res_ref[...]
    o_ref[...] = acc.astype(o_ref.dtype)


def _mm_tiles(m, k, n, has_res, row_period):
    for tm, tn in ((1024, 512), (512, 512), (512, 256), (256, 256), (256, 128), (128, 128),
                   (64, 128), (32, 128), (16, 128), (8, 128)):
        tm = min(tm, m)
        if m % tm or n % tn or row_period % tm:
            continue
        planned = (2 * k * tn * 4 + k * tn * 2 + 2 * tm * k * 2
                   + (2 + 2 * has_res + 1) * tm * tn * 4)
        if planned <= VMEM_BUDGET_BYTES:
            return tm, tn, planned
    raise ValueError(f"no matmul tiling for {(m, k, n)}")


def _matmul(x, w, n_out, *, w_col0=0, res=None, rope=None, out_dtype=F32):
    m, k = x.shape
    tm, tn, planned = _mm_tiles(m, k, n_out, res is not None, m if rope is None else rope[0].shape[0])
    assert w_col0 % tn == 0
    col0 = w_col0 // tn
    in_specs = [pl.BlockSpec((tm, k), lambda j, i: (i, 0)),
                pl.BlockSpec((k, tn), lambda j, i: (0, j + col0))]
    args = [x, w]
    if res is not None:
        in_specs.append(pl.BlockSpec((tm, tn), lambda j, i: (i, j)))
        args.append(res)
    if rope is not None:
        cos, sin_signed = rope
        p_tiles = cos.shape[0] // tm
        assert cos.shape[0] % tm == 0 and tn % LANES == 0
        in_specs += [pl.BlockSpec((tm, LANES), lambda j, i: (i % p_tiles, 0))] * 2
        args += [cos, sin_signed]
    return pl.pallas_call(
        functools.partial(_mm_kernel, has_res=res is not None, has_rope=rope is not None),
        grid=(n_out // tn, m // tm),
        in_specs=in_specs,
        out_specs=pl.BlockSpec((tm, tn), lambda j, i: (i, j)),
        out_shape=jax.ShapeDtypeStruct((m, n_out), out_dtype),
        scratch_shapes=[pltpu.VMEM((k, tn), BF16)],
        compiler_params=_params(planned, 2),
        name="matmul",
    )(*args)


def _gla_gate_norm(o, r, gh):
    y = o * lax.rsqrt(jnp.mean(o * o, axis=-1, keepdims=True) + EPS) * gh
    return y * (r * _sigmoid(r))


def _gla_prompt_kernel(q_ref, k_ref, v_ref, r_ref, a_ref, wa_ref, ba_ref, gh_ref,
                       o_ref, s_ref, *, dk):
    c = pl.program_id(2)

    @pl.when(c == 0)
    def _():
        s_ref[...] = jnp.zeros_like(s_ref)

    n_c = q_ref.shape[0]
    dv = v_ref.shape[1]
    z = jnp.dot(a_ref[...], wa_ref[...], precision=HIGHEST, preferred_element_type=F32) + ba_ref[...]
    g = _log_sigmoid(z) / GLA_GATE_TEMP
    row = lax.broadcasted_iota(jnp.int32, (n_c, n_c), 0)
    col = lax.broadcasted_iota(jnp.int32, (n_c, n_c), 1)
    causal = col <= row
    b = jnp.dot(causal.astype(F32), g, precision=HIGHEST, preferred_element_type=F32)
    b_last = b[n_c - 1:n_c, :]
    b_last_col = _tn(g, jnp.ones((n_c, LANES), F32), precision=HIGHEST)

    q = q_ref[...] * (dk ** -0.5)
    k = k_ref[...]
    v = v_ref[...].astype(BF16)
    qe = (q * jnp.exp(b)).astype(BF16)
    kd = (k * jnp.exp(-b)).astype(BF16)
    kr = (k * jnp.exp(b_last - b)).astype(BF16)
    att = jnp.where(causal, _nt(qe, kd), 0.0).astype(BF16)
    s_old = s_ref[0, 0]
    o = (jnp.dot(att, v, preferred_element_type=F32)
         + jnp.dot(qe, s_old.astype(BF16), preferred_element_type=F32))
    upd = _tn(kr, v)
    decay = jnp.exp(b_last_col)
    for t in range(dv // LANES):
        sl = slice(t * LANES, (t + 1) * LANES)
        s_ref[0, 0, :, sl] = s_old[:, sl] * decay + upd[:, sl]
    o_ref[...] = _gla_gate_norm(o, r_ref[...], gh_ref[...]).astype(o_ref.dtype)


def _gla_prompt(proj, a_pad, wa_pad, b_a2, g_head, batch, seq, heads, dk, dv):
    assert seq % GLA_CHUNK == 0 and dv == 2 * dk
    n_chunks = seq // GLA_CHUNK
    m = batch * seq
    rows = lambda b, h, c: b * n_chunks + c
    planned = 2 * (3 * GLA_CHUNK * dk + 3 * GLA_CHUNK * dv + 2 * dk * dv) * 4 + 6 * dk * dv * 4
    return pl.pallas_call(
        functools.partial(_gla_prompt_kernel, dk=dk),
        grid=(batch, heads, n_chunks),
        in_specs=[
            pl.BlockSpec((GLA_CHUNK, dk), lambda b, h, c: (rows(b, h, c), h)),
            pl.BlockSpec((GLA_CHUNK, dk), lambda b, h, c: (rows(b, h, c), heads + h)),
            pl.BlockSpec((GLA_CHUNK, dv), lambda b, h, c: (rows(b, h, c), heads + h)),
            pl.BlockSpec((GLA_CHUNK, dv), lambda b, h, c: (rows(b, h, c), 2 * heads + h)),
            pl.BlockSpec((GLA_CHUNK, LANES), lambda b, h, c: (rows(b, h, c), 0)),
            pl.BlockSpec((LANES, dk), lambda b, h, c: (0, h)),
            pl.BlockSpec((1, dk), lambda b, h, c: (0, h)),
            pl.BlockSpec((1, dv), lambda b, h, c: (0, 0)),
        ],
        out_specs=[
            pl.BlockSpec((GLA_CHUNK, dv), lambda b, h, c: (rows(b, h, c), h)),
            pl.BlockSpec((1, 1, dk, dv), lambda b, h, c: (b, h, 0, 0)),
        ],
        out_shape=[jax.ShapeDtypeStruct((m, heads * dv), BF16),
                   jax.ShapeDtypeStruct((batch, heads, dk, dv), F32)],
        compiler_params=_params(planned, 3),
        name="gla_prompt",
    )(proj, proj, proj, proj, a_pad, wa_pad, b_a2, g_head)


def _gla_sample_kernel(q_ref, k_ref, v_ref, r_ref, a_ref, wa_ref, ba_ref, gh_ref, s0_ref,
                       o_ref, s_ref, *, dk):
    a8 = jnp.broadcast_to(a_ref[0], (SUBLANES, LANES))
    z = jnp.dot(a8, wa_ref[...], precision=HIGHEST, preferred_element_type=F32)[0:1] + ba_ref[...]
    g = _log_sigmoid(z) / GLA_GATE_TEMP
    q = q_ref[0] * (dk ** -0.5)
    k = k_ref[0]
    stacked = jnp.concatenate([q, k, g, jnp.zeros((SUBLANES - 3, q.shape[1]), F32)], axis=0)
    cols = stacked.T
    qc, kc, gc = cols[:, 0:1], cols[:, 1:2], cols[:, 2:3]
    s_new = jnp.exp(gc) * s0_ref[0, 0, 0] + kc * v_ref[0]
    s_ref[0, 0] = s_new
    o = jnp.sum(qc * s_new, axis=0, keepdims=True)
    o_ref[0] = _gla_gate_norm(o, r_ref[0], gh_ref[...]).astype(o_ref.dtype)


def _gla_sample(proj, a_pad, wa_pad, b_a2, g_head, state, layer, heads, dk, dv):
    n_seq = proj.shape[0]
    planned = 2 * 2 * dk * dv * 4 + 4 * dk * dv * 4
    return pl.pallas_call(
        functools.partial(_gla_sample_kernel, dk=dk),
        grid=(n_seq, heads),
        in_specs=[
            pl.BlockSpec((1, 1, dk), lambda b, h: (b, 0, h)),
            pl.BlockSpec((1, 1, dk), lambda b, h: (b, 0, heads + h)),
            pl.BlockSpec((1, 1, dv), lambda b, h: (b, 0, heads + h)),
            pl.BlockSpec((1, 1, dv), lambda b, h: (b, 0, 2 * heads + h)),
            pl.BlockSpec((1, 1, LANES), lambda b, h: (b, 0, 0)),
            pl.BlockSpec((LANES, dk), lambda b, h: (0, h)),
            pl.BlockSpec((1, dk), lambda b, h: (0, h)),
            pl.BlockSpec((1, dv), lambda b, h: (0, 0)),
            pl.BlockSpec((1, 1, 1, dk, dv), lambda b, h: (layer, b, h, 0, 0)),
        ],
        out_specs=[
            pl.BlockSpec((1, 1, dv), lambda b, h: (b, 0, h)),
            pl.BlockSpec((1, 1, dk, dv), lambda b, h: (b, h, 0, 0)),
        ],
        out_shape=[jax.ShapeDtypeStruct((n_seq, 1, heads * dv), BF16),
                   jax.ShapeDtypeStruct((n_seq, heads, dk, dv), F32)],
        compiler_params=_params(planned, 2),
        name="gla_sample",
    )(proj, proj, proj, proj, a_pad, wa_pad, b_a2, g_head, state)


def _conv_gate(ua, ug, wca, wcg, bca, bcg):
    ca = bca + wca[0:1] * ua[0] + wca[1:2] * ua[1] + wca[2:3] * ua[2]
    cg = bcg + wcg[0:1] * ug[0] + wcg[1:2] * ug[1] + wcg[2:3] * ug[2]
    return ca * _sigmoid(ca) * cg


def _up_prompt_kernel(x_ref, wa_ref, wg_ref, wca_ref, wcg_ref, bca_ref, bcg_ref,
                      act_ref, st_ref, wba_ref, wbg_ref, ua_ref, ug_ref, *, tiles_per_seq):
    i = pl.program_id(1)
    tm = x_ref.shape[0]

    @pl.when(i == 0)
    def _():
        _cast_weight_tile(wa_ref, wba_ref)
        _cast_weight_tile(wg_ref, wbg_ref)

    @pl.when(i % tiles_per_seq == 0)
    def _():
        ua_ref[0:SUBLANES, :] = jnp.zeros((SUBLANES, ua_ref.shape[1]), F32)
        ug_ref[0:SUBLANES, :] = jnp.zeros((SUBLANES, ug_ref.shape[1]), F32)

    x = x_ref[...]
    ua_ref[SUBLANES:, :] = jnp.dot(x, wba_ref[...], preferred_element_type=F32)
    ug_ref[SUBLANES:, :] = jnp.dot(x, wbg_ref[...], preferred_element_type=F32)
    taps = lambda u_ref: tuple(u_ref[pl.ds(SUBLANES - 2 + d, tm), :] for d in range(3))
    act = _conv_gate(taps(ua_ref), taps(ug_ref), wca_ref[...], wcg_ref[...], bca_ref[...], bcg_ref[...])
    act_ref[...] = act.astype(act_ref.dtype)
    st_ref[0, 0] = ua_ref[pl.ds(tm + SUBLANES - 2, 2), :]
    st_ref[0, 1] = ug_ref[pl.ds(tm + SUBLANES - 2, 2), :]
    ua_ref[0:SUBLANES, :] = ua_ref[pl.ds(tm, SUBLANES), :]
    ug_ref[0:SUBLANES, :] = ug_ref[pl.ds(tm, SUBLANES), :]


def _up_tiles(m, k, f, seq):
    for tm, tn in ((1024, 256), (512, 256), (512, 128), (256, 128), (128, 128)):
        if seq % tm or f % tn:
            continue
        planned = (2 * 2 * k * tn * 4 + 2 * k * tn * 2 + 2 * tm * k * 2
                   + 2 * (tm + SUBLANES) * tn * 4 + 2 * tm * tn * 2 + 4 * tm * tn * 4)
        if planned <= VMEM_BUDGET_BYTES:
            return tm, tn, planned
    raise ValueError(f"no up-projection tiling for {(m, k, f)}")


def _up_prompt(xn, w_up, w_conv, b_conv, batch, seq):
    m, k = xn.shape
    f = w_up.shape[1] // 2
    assert w_conv.shape[0] == 3
    tm, tn, planned = _up_tiles(m, k, f, seq)
    g0 = f // tn
    tiles_per_seq = seq // tm
    act, st = pl.pallas_call(
        functools.partial(_up_prompt_kernel, tiles_per_seq=tiles_per_seq),
        grid=(f // tn, m // tm),
        in_specs=[
            pl.BlockSpec((tm, k), lambda j, i: (i, 0)),
            pl.BlockSpec((k, tn), lambda j, i: (0, j)),
            pl.BlockSpec((k, tn), lambda j, i: (0, j + g0)),
            pl.BlockSpec((3, tn), lambda j, i: (0, j)),
            pl.BlockSpec((3, tn), lambda j, i: (0, j + g0)),
            pl.BlockSpec((1, tn), lambda j, i: (0, j)),
            pl.BlockSpec((1, tn), lambda j, i: (0, j + g0)),
        ],
        out_specs=[
            pl.BlockSpec((tm, tn), lambda j, i: (i, j)),
            pl.BlockSpec((1, 2, 2, tn), lambda j, i: (i // tiles_per_seq, 0, 0, j)),
        ],
        out_shape=[jax.ShapeDtypeStruct((m, f), BF16),
                   jax.ShapeDtypeStruct((batch, 2, 2, f), F32)],
        scratch_shapes=[pltpu.VMEM((k, tn), BF16), pltpu.VMEM((k, tn), BF16),
                        pltpu.VMEM((tm + SUBLANES, tn), F32), pltpu.VMEM((tm + SUBLANES, tn), F32)],
        compiler_params=_params(planned, 2),
        name="up_conv_prompt",
    )(xn, w_up, w_up, w_conv, w_conv, b_conv, b_conv)
    return act, st.transpose(0, 2, 1, 3).reshape(batch, 2, 2 * f)


def _up_sample_kernel(x_ref, wa_ref, wg_ref, wca_ref, wcg_ref, bca_ref, bcg_ref, sa_ref, sg_ref,
                      act_ref, u_ref):
    x = x_ref[...]
    ua = jnp.dot(x, wa_ref[...].astype(BF16), preferred_element_type=F32)
    ug = jnp.dot(x, wg_ref[...].astype(BF16), preferred_element_type=F32)
    act = _conv_gate((sa_ref[0], sa_ref[1], ua), (sg_ref[0], sg_ref[1], ug),
                     wca_ref[...], wcg_ref[...], bca_ref[...], bcg_ref[...])
    act_ref[...] = act.astype(act_ref.dtype)
    u_ref[0] = ua
    u_ref[1] = ug


def _up_sample(xn, w_up, w_conv, b_conv, conv_rows):
    n_seq, k = xn.shape
    f = w_up.shape[1] // 2
    tn = _pick(f, (256, 128))
    g0 = f // tn
    planned = 2 * 2 * k * tn * 4 + 2 * k * tn * 2 + 2 * n_seq * k * 2
    act, u = pl.pallas_call(
        _up_sample_kernel,
        grid=(f // tn,),
        in_specs=[
            pl.BlockSpec((n_seq, k), lambda j: (0, 0)),
            pl.BlockSpec((k, tn), lambda j: (0, j)),
            pl.BlockSpec((k, tn), lambda j: (0, j + g0)),
            pl.BlockSpec((3, tn), lambda j: (0, j)),
            pl.BlockSpec((3, tn), lambda j: (0, j + g0)),
            pl.BlockSpec((1, tn), lambda j: (0, j)),
            pl.BlockSpec((1, tn), lambda j: (0, j + g0)),
            pl.BlockSpec((2, n_seq, tn), lambda j: (0, 0, j)),
            pl.BlockSpec((2, n_seq, tn), lambda j: (0, 0, j + g0)),
        ],
        out_specs=[
            pl.BlockSpec((n_seq, tn), lambda j: (0, j)),
            pl.BlockSpec((2, n_seq, tn), lambda j: (0, 0, j)),
        ],
        out_shape=[jax.ShapeDtypeStruct((n_seq, f), BF16),
                   jax.ShapeDtypeStruct((2, n_seq, f), F32)],
        compiler_params=_params(planned, 1),
        name="up_conv_sample",
    )(xn, w_up, w_up, w_conv, w_conv, b_conv, b_conv, conv_rows, conv_rows)
    return act, u.transpose(1, 0, 2).reshape(n_seq, 2 * f)


def _moba_prompt_kernel(q_ref, k_ref, v_ref, o_ref, kb_ref, vt_ref, mean_ref, p_ref,
                        *, n_blocks, scale):
    i = pl.program_id(2)
    blk = q_ref.shape[0]
    hd = q_ref.shape[1]

    @pl.when(i == 0)
    def _():
        kf = k_ref[...]
        kb_ref[...] = kf.astype(BF16)
        mean_ref[...] = jnp.zeros_like(mean_ref)
        for j in range(n_blocks):
            mean_ref[j:j + 1, :] = jnp.mean(kf[j * blk:(j + 1) * blk, :], axis=0, keepdims=True)
        vt_ref[0:hd, :] = v_ref[...].T.astype(BF16)
        vt_ref[hd:2 * hd, :] = jnp.ones((hd, vt_ref.shape[1]), BF16)

    qf = q_ref[...]
    gate_t = _nt(mean_ref[...], qf, precision=HIGHEST)
    blk_id = lax.broadcasted_iota(jnp.int32, gate_t.shape, 0)
    past = blk_id < i
    selected = []
    for j in range(n_blocks):
        gj = gate_t[j:j + 1, :]
        ahead = jnp.where(past & ((gate_t > gj) | ((gate_t == gj) & (blk_id < j))), 1.0, 0.0)
        selected.append(jnp.sum(ahead, axis=0, keepdims=True) < MOBA_TOPK)

    qs = (qf * scale).astype(BF16)
    s_t = _nt(kb_ref[...], qs)
    key_l = lax.broadcasted_iota(jnp.int32, (blk, blk), 0)
    qry_l = lax.broadcasted_iota(jnp.int32, (blk, blk), 1)
    causal = jnp.where(key_l <= qry_l, 1, 0)
    masked = []
    for j in range(n_blocks):
        is_past = (j < i).astype(jnp.int32)
        is_own = (j == i).astype(jnp.int32)
        allow = jnp.where(selected[j], 1, 0) * is_past + causal * is_own
        masked.append(jnp.where(allow > 0, s_t[j * blk:(j + 1) * blk, :], NEG_BIG))
    mx = masked[0].max(axis=0, keepdims=True)
    for j in range(1, n_blocks):
        mx = jnp.maximum(mx, masked[j].max(axis=0, keepdims=True))
    for j in range(n_blocks):
        p_ref[j * blk:(j + 1) * blk, :] = jnp.exp(masked[j] - mx).astype(BF16)
    o_t = jnp.dot(vt_ref[...], p_ref[...], preferred_element_type=F32)
    o_t = o_t[0:hd, :] / o_t[hd:hd + 1, :]
    o_ref[...] = o_t.T.astype(o_ref.dtype)


def _moba_prompt(q, k, v, batch, seq, heads, hd):
    assert seq % MOBA_BLOCK == 0 and hd == LANES
    n_blocks = seq // MOBA_BLOCK
    mean_rows = -(-n_blocks // SUBLANES) * SUBLANES
    planned = (2 * 2 * seq * hd * 4 + seq * hd * 2 + 2 * hd * seq * 2 + seq * MOBA_BLOCK * 2
               + 4 * seq * MOBA_BLOCK * 4)
    return pl.pallas_call(
        functools.partial(_moba_prompt_kernel, n_blocks=n_blocks, scale=hd ** -0.5),
        grid=(batch, heads, n_blocks),
        in_specs=[
            pl.BlockSpec((MOBA_BLOCK, hd), lambda b, h, i: (b * n_blocks + i, h)),
            pl.BlockSpec((seq, hd), lambda b, h, i: (b, h)),
            pl.BlockSpec((seq, hd), lambda b, h, i: (b, h)),
        ],
        out_specs=pl.BlockSpec((MOBA_BLOCK, hd), lambda b, h, i: (b * n_blocks + i, h)),
        out_shape=jax.ShapeDtypeStruct((batch * seq, heads * hd), BF16),
        scratch_shapes=[pltpu.VMEM((seq, hd), BF16), pltpu.VMEM((2 * hd, seq), BF16),
                        pltpu.VMEM((mean_rows, hd), F32), pltpu.VMEM((seq, MOBA_BLOCK), BF16)],
        compiler_params=_params(planned, 3),
        name="moba_prompt",
    )(q, k, v)


def _moba_select_kernel(pt_ref, ck_ref, q_ref, seg_ref, idx_ref, mean_ref,
                        *, pages_per_block, n_blocks):
    p = pl.program_id(1)
    blk = p // pages_per_block
    grp = blk // SUBLANES

    @pl.when(p == 0)
    def _():
        mean_ref[...] = jnp.zeros_like(mean_ref)

    page_sum = jnp.sum(ck_ref[0], axis=0, keepdims=True)
    row = lax.broadcasted_iota(jnp.int32, (SUBLANES, page_sum.shape[1]), 0)
    mean_ref[grp] = mean_ref[grp] + jnp.where(row == blk % SUBLANES, page_sum, 0.0)

    @pl.when(p == pl.num_programs(1) - 1)
    def _():
        tokens = pages_per_block * ck_ref.shape[1]
        means = mean_ref[...].reshape(n_blocks, mean_ref.shape[2]) * (1.0 / tokens)
        gate = jnp.dot(means * q_ref[0], seg_ref[...], precision=HIGHEST,
                       preferred_element_type=F32)
        blk_id = lax.broadcasted_iota(jnp.int32, gate.shape, 0)
        out_row = lax.broadcasted_iota(jnp.int32, (SUBLANES, gate.shape[1]), 0)
        picks = jnp.zeros((SUBLANES, gate.shape[1]), jnp.int32)
        for t in range(min(MOBA_TOPK, n_blocks)):
            best = gate.max(axis=0, keepdims=True)
            first = jnp.where(gate == best, blk_id, n_blocks).min(axis=0, keepdims=True)
            picks = jnp.where(out_row == t, first, picks)
            gate = jnp.where(blk_id == first, -jnp.inf, gate)
        idx_ref[0] = picks


def _moba_select(page_table, cache_k3, q, heads, hd):
    n_seq, n_pages = page_table.shape
    page = cache_k3.shape[1]
    pages_per_block = MOBA_BLOCK // page
    assert n_pages % pages_per_block == 0
    n_blocks = n_pages // pages_per_block
    assert n_blocks % SUBLANES == 0 and heads <= LANES
    width = heads * hd
    seg = (jnp.arange(width)[:, None] // hd == jnp.arange(LANES)[None, :]).astype(F32)
    planned = 2 * page * width * 4 + n_blocks * width * 4 * 3 + 2 * width * LANES * 4
    grid_spec = pltpu.PrefetchScalarGridSpec(
        num_scalar_prefetch=1,
        grid=(n_seq, n_pages),
        in_specs=[
            pl.BlockSpec((1, page, width), lambda b, p, pt: (pt[b, p], 0, 0)),
            pl.BlockSpec((1, 1, width), lambda b, p, pt: (b, 0, 0)),
            pl.BlockSpec((width, LANES), lambda b, p, pt: (0, 0)),
        ],
        out_specs=pl.BlockSpec((1, SUBLANES, LANES), lambda b, p, pt: (b, 0, 0)),
        scratch_shapes=[pltpu.VMEM((n_blocks // SUBLANES, SUBLANES, width), F32)],
    )
    return pl.pallas_call(
        functools.partial(_moba_select_kernel, pages_per_block=pages_per_block, n_blocks=n_blocks),
        grid_spec=grid_spec,
        out_shape=jax.ShapeDtypeStruct((n_seq, SUBLANES, LANES), jnp.int32),
        compiler_params=_params(planned, 2),
        name="moba_select",
    )(page_table, cache_k3, q, seg)


def _moba_sample_kernel(pt_ref, idx_ref, *refs, n_sel, scale):
    k_refs = refs[:n_sel]
    v_refs = refs[n_sel:2 * n_sel]
    q_ref, kn_ref, vn_ref, o_ref = refs[2 * n_sel:]
    q8 = jnp.broadcast_to(q_ref[0] * scale, (SUBLANES, q_ref.shape[2]))
    scores = [_nt(q8, k_ref[0], precision=HIGHEST) for k_ref in k_refs]
    s_own = jnp.sum(q8 * kn_ref[0], axis=-1, keepdims=True)
    mx = s_own
    for s in scores:
        mx = jnp.maximum(mx, s.max(axis=-1, keepdims=True))
    p_own = jnp.exp(s_own - mx)
    denom = p_own
    acc = p_own * vn_ref[0]
    for s, v_ref in zip(scores, v_refs):
        p = jnp.exp(s - mx)
        denom = denom + jnp.sum(p, axis=-1, keepdims=True)
        acc = acc + jnp.dot(p, v_ref[0], precision=HIGHEST, preferred_element_type=F32)
    o_ref[0] = (acc / denom)[0:1].astype(o_ref.dtype)


def _moba_sample(page_table, picks, cache_k3, cache_v3, q, k_new, v_new, heads, hd):
    n_seq, n_pages = page_table.shape
    page = cache_k3.shape[1]
    pages_per_block = MOBA_BLOCK // page
    n_top = picks.shape[1]
    n_sel = n_top * pages_per_block

    def page_spec(t, pp):
        return pl.BlockSpec(
            (1, page, hd),
            lambda b, h, pt, ix: (pt[b, ix[b, t, h] * pages_per_block + pp], 0, h))

    page_specs = [page_spec(t, pp) for t in range(n_top) for pp in range(pages_per_block)]
    vec_spec = pl.BlockSpec((1, 1, hd), lambda b, h, pt, ix: (b, 0, h))
    planned = 2 * 2 * n_sel * page * hd * 4 * 2
    grid_spec = pltpu.PrefetchScalarGridSpec(
        num_scalar_prefetch=2,
        grid=(n_seq, heads),
        in_specs=page_specs + page_specs + [vec_spec, vec_spec, vec_spec],
        out_specs=vec_spec,
    )
    return pl.pallas_call(
        functools.partial(_moba_sample_kernel, n_sel=n_sel, scale=hd ** -0.5),
        grid_spec=grid_spec,
        out_shape=jax.ShapeDtypeStruct((n_seq, 1, heads * hd), BF16),
        compiler_params=_params(planned, 2),
        name="moba_sample",
    )(page_table, picks, *([cache_k3] * n_sel), *([cache_v3] * n_sel), q, k_new, v_new)


def _rope_tables(pos, hd):
    half = hd // 2
    inv = ROPE_THETA ** (-jnp.arange(half, dtype=F32) / half)
    ang = pos[:, None] * inv[None, :]
    cos, sin = jnp.cos(ang), jnp.sin(ang)
    return jnp.concatenate([cos, cos], axis=-1), jnp.concatenate([-sin, sin], axis=-1)


def kernel(x_prompt, x_sample, cache_k, cache_v, page_table, state_gla, state_conv, g_mix, w_in_a,
           w_a2, b_a2, g_gla_head, w_out_a, g_kv, w_kv, w_q, w_out_b, g_ffn, w_up, w_conv, b_conv,
           w_down, g_final):
    n_b, seq, d = x_prompt.shape
    n_s, t_s, _ = x_sample.shape
    assert t_s == 1
    _, _, gla_h, dk, dv = state_gla.shape
    _, page, att_h, hd = cache_k.shape
    n_pages = page_table.shape[1]
    past_len = n_pages * page
    rank = w_a2.shape[1]
    qk_cols, v_cols = gla_h * dk, gla_h * dv
    proj_cols = 2 * qk_cols + 2 * v_cols
    da = att_h * hd
    assert rank <= LANES and len(w_in_a) == 1 and len(w_q) == 1

    xp = x_prompt.reshape(n_b * seq, d)
    xs = x_sample.reshape(n_s, d)
    cache_k3 = cache_k.reshape(cache_k.shape[0], page, da)
    cache_v3 = cache_v.reshape(cache_v.shape[0], page, da)
    rope_p = _rope_tables(jnp.arange(seq, dtype=F32), hd)
    rope_s = tuple(jnp.tile(t, (n_s, 1)) for t in _rope_tables(past_len + jnp.arange(t_s, dtype=F32), hd))

    w_a1_pad = jnp.pad(w_in_a[0][:, proj_cols:], ((0, 0), (0, LANES - rank)))
    w_a2_pad = jnp.pad(w_a2[0], ((0, LANES - rank), (0, 0)))
    conv_rows = state_conv.transpose(0, 2, 1, 3)

    def layer0(h, prompt):
        (hn,) = _rmsnorm(h, g_mix[0:1], BF16)
        proj = _matmul(hn, w_in_a[0], proj_cols)
        a_pad = _matmul(hn, w_a1_pad, LANES)
        if prompt:
            og, s = _gla_prompt(proj, a_pad, w_a2_pad, b_a2[0:1], g_gla_head[0:1], n_b, seq, gla_h, dk, dv)
        else:
            og, s = _gla_sample(proj.reshape(n_s, 1, proj_cols), a_pad.reshape(n_s, 1, LANES), w_a2_pad,
                                b_a2[0:1], g_gla_head[0:1], state_gla, 0, gla_h, dk, dv)
            og = og.reshape(n_s, v_cols)
        return _matmul(og, w_out_a[0], d, res=h), s

    def conv_ffn(h, layer, prompt):
        (hn,) = _rmsnorm(h, g_ffn[layer:layer + 1], BF16)
        if prompt:
            act, st = _up_prompt(hn, w_up[layer], w_conv[layer], b_conv[layer:layer + 1], n_b, seq)
        else:
            act, u = _up_sample(hn, w_up[layer], w_conv[layer], b_conv[layer:layer + 1], conv_rows[layer])
            st = jnp.stack([state_conv[layer][:, 1, :], u], axis=1)
        return _matmul(act, w_down[layer], d, res=h), st

    def layer1(h, prompt):
        hn, kvn = _rmsnorm(h, jnp.stack([g_mix[1], g_kv]), BF16)
        rope = rope_p if prompt else rope_s
        k = _matmul(kvn, w_kv, da, rope=rope)
        v = _matmul(kvn, w_kv, da, w_col0=da)
        q = _matmul(hn, w_q[0], da, rope=rope)
        if prompt:
            att = _moba_prompt(q, k, v, n_b, seq, att_h, hd)
        else:
            q3, k3, v3 = (t.reshape(n_s, 1, da) for t in (q, k, v))
            picks = _moba_select(page_table, cache_k3, q3, att_h, hd)
            picks = picks[:, :min(MOBA_TOPK, n_pages * page // MOBA_BLOCK), :att_h]
            att = _moba_sample(page_table, picks, cache_k3, cache_v3, q3, k3, v3, att_h, hd)
            att = att.reshape(n_s, da)
        return _matmul(att, w_out_b[0], d, res=h), k, v

    def trunk(h, prompt):
        h, s_gla = layer0(h, prompt)
        h, st0 = conv_ffn(h, 0, prompt)
        h, k, v = layer1(h, prompt)
        h, st1 = conv_ffn(h, 1, prompt)
        (y,) = _rmsnorm(h, g_final[None, :], F32)
        return y, k, v, s_gla[None], jnp.stack([st0, st1])

    y_p, k_p, v_p, gla_p, conv_p = trunk(xp, True)
    y_s, k_s, v_s, gla_s, conv_s = trunk(xs, False)
    return (y_p.reshape(n_b, seq, d), y_s.reshape(n_s, t_s, d),
            k_p.reshape(n_b, seq, att_h, hd), v_p.reshape(n_b, seq, att_h, hd), gla_p, conv_p,
            k_s.reshape(n_s, t_s, att_h, hd), v_s.reshape(n_s, t_s, att_h, hd), gla_s, conv_s)
```

```python
import functools

import jax
import jax.numpy as jnp
from jax import lax
from jax.experimental import pallas as pl
from jax.experimental.pallas import tpu as pltpu

F32 = jnp.float32
BF16 = jnp.bfloat16
HIGHEST = lax.Precision.HIGHEST

GLA_GATE_TEMP = 16.0
GLA_CHUNK = 64
MOBA_BLOCK = 256
MOBA_TOPK = 3
ROPE_THETA = 10000.0
EPS = 1e-6

LANES = 128
SUBLANES = 8
VMEM_PHYSICAL_BYTES = 64 * 1024 * 1024
VMEM_BUDGET_BYTES = 44 * 1024 * 1024
VMEM_COMPILER_SLACK_BYTES = 8 * 1024 * 1024
NEG_BIG = -1e30


def _vmem_limit(planned_bytes):
    return int(min(planned_bytes + VMEM_COMPILER_SLACK_BYTES, VMEM_PHYSICAL_BYTES - 4 * 1024 * 1024))


def _params(planned_bytes, n_axes):
    return pltpu.CompilerParams(
        dimension_semantics=("arbitrary",) * n_axes,
        vmem_limit_bytes=_vmem_limit(planned_bytes),
    )


def _nt(a, b, precision=None):
    return lax.dot_general(a, b, (((1,), (1,)), ((), ())), precision=precision,
                           preferred_element_type=F32)


def _tn(a, b, precision=None):
    return lax.dot_general(a, b, (((0,), (0,)), ((), ())), precision=precision,
                           preferred_element_type=F32)


def _sigmoid(x):
    return 1.0 / (1.0 + jnp.exp(-x))


def _log_sigmoid(z):
    return jnp.minimum(z, 0.0) - jnp.log1p(jnp.exp(-jnp.abs(z)))


def _pick(n, candidates):
    for c in candidates:
        if n % c == 0:
            return c
    return n


def _norm_kernel(x_ref, g_ref, *o_refs):
    x = x_ref[...]
    y = x * lax.rsqrt(jnp.mean(x * x, axis=-1, keepdims=True) + EPS)
    for i, o_ref in enumerate(o_refs):
        o_ref[...] = (y * g_ref[i:i + 1, :]).astype(o_ref.dtype)


def _rmsnorm(x, gains, out_dtype):
    m, d = x.shape
    n_g = gains.shape[0]
    tm = _pick(m, (256, 128, 64, 32, 16, 8))
    planned = 2 * tm * d * 4 * (1 + n_g)
    outs = pl.pallas_call(
        _norm_kernel,
        grid=(m // tm,),
        in_specs=[pl.BlockSpec((tm, d), lambda i: (i, 0)),
                  pl.BlockSpec((n_g, d), lambda i: (0, 0))],
        out_specs=[pl.BlockSpec((tm, d), lambda i: (i, 0)) for _ in range(n_g)],
        out_shape=[jax.ShapeDtypeStruct((m, d), out_dtype) for _ in range(n_g)],
        compiler_params=_params(planned, 1),
        name="rmsnorm",
    )(x, gains)
    return outs


def _rope_tile(acc, cos, sin_signed):
    heads = []
    for c in range(acc.shape[1] // LANES):
        xc = acc[:, c * LANES:(c + 1) * LANES]
        heads.append(xc * cos + pltpu.roll(xc, LANES // 2, 1) * sin_signed)
    return heads[0] if len(heads) == 1 else jnp.concatenate(heads, axis=1)


def _cast_weight_tile(w_ref, wb_ref):
    k = w_ref.shape[0]
    ck = _pick(k, (512, 256, 128))

    def body(i, carry):
        r = pl.multiple_of(i * ck, ck)
        wb_ref[pl.ds(r, ck), :] = w_ref[pl.ds(r, ck), :].astype(BF16)
        return carry

    lax.fori_loop(0, k // ck, body, 0)


def _mm_kernel(*refs, has_res, has_rope):
    x_ref, w_ref = refs[0], refs[1]
    pos = 2
    res_ref = cos_ref = sin_ref = None
    if has_res:
        res_ref = refs[pos]
        pos += 1
    if has_rope:
        cos_ref, sin_ref = refs[pos], refs[pos + 1]
        pos += 2
    o_ref, wb_ref = refs[pos], refs[pos + 1]

    @pl.when(pl.program_id(1) == 0)
    def _():
        _cast_weight_tile(w_ref, wb_ref)

    acc = jnp.dot(x_ref[...], wb_ref[...], preferred_element_type=F32)
    if has_rope:
        acc = _rope_tile(acc, cos_ref[...], sin_ref[...])
    if has_res:
        acc = acc + res_ref[...]
    o_ref[...] = acc.astype(o_ref.dtype)


def _mm_tiles(m, k, n, has_res, row_period):
    for tm, tn in ((1024, 512), (512, 512), (512, 256), (256, 256), (256, 128), (128, 128),
                   (64, 128), (32, 128), (16, 128), (8, 128)):
        tm = min(tm, m)
        if m % tm or n % tn or row_period % tm:
            continue
        planned = (2 * k * tn * 4 + k * tn * 2 + 2 * tm * k * 2
                   + (2 + 2 * has_res + 1) * tm * tn * 4)
        if planned <= VMEM_BUDGET_BYTES:
            return tm, tn, planned
    raise ValueError(f"no matmul tiling for {(m, k, n)}")


def _matmul(x, w, layer, n_out, *, w_col0=0, k_split=1, res=None, rope=None, out_dtype=F32):
    if k_split > 1:
        assert rope is None
        for part in range(k_split):
            res = _matmul_part(x, w, layer, n_out, w_col0, k_split, part, res, None,
                               out_dtype if part == k_split - 1 else F32)
        return res
    return _matmul_part(x, w, layer, n_out, w_col0, 1, 0, res, rope, out_dtype)


def _matmul_part(x, w, layer, n_out, w_col0, k_split, part, res, rope, out_dtype):
    m = x.shape[0]
    k = x.shape[1] // k_split
    assert x.shape[1] % k_split == 0 and k % LANES == 0
    tm, tn, planned = _mm_tiles(m, k, n_out, res is not None, m if rope is None else rope[0].shape[0])
    assert w_col0 % tn == 0
    col0 = w_col0 // tn
    in_specs = [pl.BlockSpec((tm, k), lambda j, i: (i, part)),
                pl.BlockSpec((None, k, tn), lambda j, i: (layer, part, j + col0))]
    args = [x, w]
    if res is not None:
        in_specs.append(pl.BlockSpec((tm, tn), lambda j, i: (i, j)))
        args.append(res)
    if rope is not None:
        cos, sin_signed = rope
        p_tiles = cos.shape[0] // tm
        assert cos.shape[0] % tm == 0 and tn % LANES == 0
        in_specs += [pl.BlockSpec((tm, LANES), lambda j, i: (i % p_tiles, 0))] * 2
        args += [cos, sin_signed]
    return pl.pallas_call(
        functools.partial(_mm_kernel, has_res=res is not None, has_rope=rope is not None),
        grid=(n_out // tn, m // tm),
        in_specs=in_specs,
        out_specs=pl.BlockSpec((tm, tn), lambda j, i: (i, j)),
        out_shape=jax.ShapeDtypeStruct((m, n_out), out_dtype),
        scratch_shapes=[pltpu.VMEM((k, tn), BF16)],
        compiler_params=_params(planned, 2),
        name="matmul",
    )(*args)


def _gla_gate_norm(o, r, gh):
    y = o * lax.rsqrt(jnp.mean(o * o, axis=-1, keepdims=True) + EPS) * gh
    return y * (r * _sigmoid(r))


def _gla_prompt_kernel(q_ref, k_ref, v_ref, r_ref, a_ref, wa_ref, ba_ref, gh_ref,
                       o_ref, s_ref, *, dk):
    c = pl.program_id(2)

    @pl.when(c == 0)
    def _():
        s_ref[...] = jnp.zeros_like(s_ref)

    n_c = q_ref.shape[0]
    dv = v_ref.shape[1]
    z = jnp.dot(a_ref[...], wa_ref[...], precision=HIGHEST, preferred_element_type=F32) + ba_ref[...]
    g = _log_sigmoid(z) / GLA_GATE_TEMP
    row = lax.broadcasted_iota(jnp.int32, (n_c, n_c), 0)
    col = lax.broadcasted_iota(jnp.int32, (n_c, n_c), 1)
    causal = col <= row
    b = jnp.dot(causal.astype(F32), g, precision=HIGHEST, preferred_element_type=F32)
    b_last = b[n_c - 1:n_c, :]
    b_last_col = _tn(g, jnp.ones((n_c, LANES), F32), precision=HIGHEST)

    q = q_ref[...] * (dk ** -0.5)
    k = k_ref[...]
    v = v_ref[...].astype(BF16)
    qe = (q * jnp.exp(b)).astype(BF16)
    kd = (k * jnp.exp(-b)).astype(BF16)
    kr = (k * jnp.exp(b_last - b)).astype(BF16)
    att = jnp.where(causal, _nt(qe, kd), 0.0).astype(BF16)
    s_old = s_ref[0, 0]
    o = (jnp.dot(att, v, preferred_element_type=F32)
         + jnp.dot(qe, s_old.astype(BF16), preferred_element_type=F32))
    upd = _tn(kr, v)
    decay = jnp.exp(b_last_col)
    for t in range(dv // LANES):
        sl = slice(t * LANES, (t + 1) * LANES)
        s_ref[0, 0, :, sl] = s_old[:, sl] * decay + upd[:, sl]
    o_ref[...] = _gla_gate_norm(o, r_ref[...], gh_ref[...]).astype(o_ref.dtype)


def _gla_prompt(proj, a_pad, wa_pad, b_a2, g_head, batch, seq, heads, dk, dv):
    assert seq % GLA_CHUNK == 0 and dv == 2 * dk
    n_chunks = seq // GLA_CHUNK
    m = batch * seq
    rows = lambda b, h, c: b * n_chunks + c
    planned = 2 * (3 * GLA_CHUNK * dk + 3 * GLA_CHUNK * dv + 2 * dk * dv) * 4 + 6 * dk * dv * 4
    return pl.pallas_call(
        functools.partial(_gla_prompt_kernel, dk=dk),
        grid=(batch, heads, n_chunks),
        in_specs=[
            pl.BlockSpec((GLA_CHUNK, dk), lambda b, h, c: (rows(b, h, c), h)),
            pl.BlockSpec((GLA_CHUNK, dk), lambda b, h, c: (rows(b, h, c), heads + h)),
            pl.BlockSpec((GLA_CHUNK, dv), lambda b, h, c: (rows(b, h, c), heads + h)),
            pl.BlockSpec((GLA_CHUNK, dv), lambda b, h, c: (rows(b, h, c), 2 * heads + h)),
            pl.BlockSpec((GLA_CHUNK, LANES), lambda b, h, c: (rows(b, h, c), 0)),
            pl.BlockSpec((LANES, dk), lambda b, h, c: (0, h)),
            pl.BlockSpec((1, dk), lambda b, h, c: (0, h)),
            pl.BlockSpec((1, dv), lambda b, h, c: (0, 0)),
        ],
        out_specs=[
            pl.BlockSpec((GLA_CHUNK, dv), lambda b, h, c: (rows(b, h, c), h)),
            pl.BlockSpec((1, 1, dk, dv), lambda b, h, c: (b, h, 0, 0)),
        ],
        out_shape=[jax.ShapeDtypeStruct((m, heads * dv), BF16),
                   jax.ShapeDtypeStruct((batch, heads, dk, dv), F32)],
        compiler_params=_params(planned, 3),
        name="gla_prompt",
    )(proj, proj, proj, proj, a_pad, wa_pad, b_a2, g_head)


def _gla_sample_kernel(q_ref, k_ref, v_ref, r_ref, a_ref, wa_ref, ba_ref, gh_ref, s0_ref,
                       o_ref, s_ref, *, dk):
    a8 = jnp.broadcast_to(a_ref[0], (SUBLANES, LANES))
    z = jnp.dot(a8, wa_ref[...], precision=HIGHEST, preferred_element_type=F32)[0:1] + ba_ref[...]
    g = _log_sigmoid(z) / GLA_GATE_TEMP
    q = q_ref[0] * (dk ** -0.5)
    k = k_ref[0]
    stacked = jnp.concatenate([q, k, g, jnp.zeros((SUBLANES - 3, q.shape[1]), F32)], axis=0)
    cols = stacked.T
    qc, kc, gc = cols[:, 0:1], cols[:, 1:2], cols[:, 2:3]
    s_new = jnp.exp(gc) * s0_ref[0, 0, 0] + kc * v_ref[0]
    s_ref[0, 0] = s_new
    o = jnp.sum(qc * s_new, axis=0, keepdims=True)
    o_ref[0] = _gla_gate_norm(o, r_ref[0], gh_ref[...]).astype(o_ref.dtype)


def _gla_sample(proj, a_pad, wa_pad, b_a2, g_head, state, layer, heads, dk, dv):
    n_seq = proj.shape[0]
    planned = 2 * 2 * dk * dv * 4 + 4 * dk * dv * 4
    return pl.pallas_call(
        functools.partial(_gla_sample_kernel, dk=dk),
        grid=(n_seq, heads),
        in_specs=[
            pl.BlockSpec((1, 1, dk), lambda b, h: (b, 0, h)),
            pl.BlockSpec((1, 1, dk), lambda b, h: (b, 0, heads + h)),
            pl.BlockSpec((1, 1, dv), lambda b, h: (b, 0, heads + h)),
            pl.BlockSpec((1, 1, dv), lambda b, h: (b, 0, 2 * heads + h)),
            pl.BlockSpec((1, 1, LANES), lambda b, h: (b, 0, 0)),
            pl.BlockSpec((LANES, dk), lambda b, h: (0, h)),
            pl.BlockSpec((1, dk), lambda b, h: (0, h)),
            pl.BlockSpec((1, dv), lambda b, h: (0, 0)),
            pl.BlockSpec((1, 1, 1, dk, dv), lambda b, h: (layer, b, h, 0, 0)),
        ],
        out_specs=[
            pl.BlockSpec((1, 1, dv), lambda b, h: (b, 0, h)),
            pl.BlockSpec((1, 1, dk, dv), lambda b, h: (b, h, 0, 0)),
        ],
        out_shape=[jax.ShapeDtypeStruct((n_seq, 1, heads * dv), BF16),
                   jax.ShapeDtypeStruct((n_seq, heads, dk, dv), F32)],
        compiler_params=_params(planned, 2),
        name="gla_sample",
    )(proj, proj, proj, proj, a_pad, wa_pad, b_a2, g_head, state)


def _conv_gate(ua, ug, wca, wcg, bca, bcg):
    ca = bca + wca[0:1] * ua[0] + wca[1:2] * ua[1] + wca[2:3] * ua[2]
    cg = bcg + wcg[0:1] * ug[0] + wcg[1:2] * ug[1] + wcg[2:3] * ug[2]
    return ca * _sigmoid(ca) * cg


def _up_prompt_kernel(x_ref, wa_ref, wg_ref, wca_ref, wcg_ref, bca_ref, bcg_ref,
                      act_ref, st_ref, wba_ref, wbg_ref, ua_ref, ug_ref, *, tiles_per_seq):
    i = pl.program_id(1)
    tm = x_ref.shape[0]

    @pl.when(i == 0)
    def _():
        _cast_weight_tile(wa_ref, wba_ref)
        _cast_weight_tile(wg_ref, wbg_ref)

    @pl.when(i % tiles_per_seq == 0)
    def _():
        ua_ref[0:SUBLANES, :] = jnp.zeros((SUBLANES, ua_ref.shape[1]), F32)
        ug_ref[0:SUBLANES, :] = jnp.zeros((SUBLANES, ug_ref.shape[1]), F32)

    sub = _pick(tm, (256, 128))
    for r0 in range(0, tm, sub):
        x = x_ref[r0:r0 + sub, :]
        ua_ref[SUBLANES + r0:SUBLANES + r0 + sub, :] = jnp.dot(x, wba_ref[...], preferred_element_type=F32)
        ug_ref[SUBLANES + r0:SUBLANES + r0 + sub, :] = jnp.dot(x, wbg_ref[...], preferred_element_type=F32)
        taps = lambda u_ref: tuple(u_ref[SUBLANES - 2 + d + r0:SUBLANES - 2 + d + r0 + sub, :] for d in range(3))
        act = _conv_gate(taps(ua_ref), taps(ug_ref), wca_ref[...], wcg_ref[...], bca_ref[...], bcg_ref[...])
        act_ref[r0:r0 + sub, :] = act.astype(act_ref.dtype)
    st_ref[0, 0] = ua_ref[pl.ds(tm + SUBLANES - 2, 2), :]
    st_ref[0, 1] = ug_ref[pl.ds(tm + SUBLANES - 2, 2), :]
    ua_ref[0:SUBLANES, :] = ua_ref[pl.ds(tm, SUBLANES), :]
    ug_ref[0:SUBLANES, :] = ug_ref[pl.ds(tm, SUBLANES), :]


def _up_tiles(m, k, f, seq):
    for tm, tn in ((1024, 256), (512, 256), (512, 128), (256, 128), (128, 128)):
        if seq % tm or f % tn:
            continue
        planned = (2 * 2 * k * tn * 4 + 2 * k * tn * 2 + 2 * tm * k * 2
                   + 2 * (tm + SUBLANES) * tn * 4 + 2 * tm * tn * 2 + 4 * tm * tn * 4)
        if planned <= VMEM_BUDGET_BYTES:
            return tm, tn, planned
    raise ValueError(f"no up-projection tiling for {(m, k, f)}")


def _up_prompt(xn, w_up, w_conv, b_conv, layer, batch, seq):
    m, k = xn.shape
    f = w_up.shape[2] // 2
    assert w_conv.shape[1] == 3
    tm, tn, planned = _up_tiles(m, k, f, seq)
    g0 = f // tn
    tiles_per_seq = seq // tm
    act, st = pl.pallas_call(
        functools.partial(_up_prompt_kernel, tiles_per_seq=tiles_per_seq),
        grid=(f // tn, m // tm),
        in_specs=[
            pl.BlockSpec((tm, k), lambda j, i: (i, 0)),
            pl.BlockSpec((None, k, tn), lambda j, i: (layer, 0, j)),
            pl.BlockSpec((None, k, tn), lambda j, i: (layer, 0, j + g0)),
            pl.BlockSpec((None, 3, tn), lambda j, i: (layer, 0, j)),
            pl.BlockSpec((None, 3, tn), lambda j, i: (layer, 0, j + g0)),
            pl.BlockSpec((None, 1, tn), lambda j, i: (layer, 0, j)),
            pl.BlockSpec((None, 1, tn), lambda j, i: (layer, 0, j + g0)),
        ],
        out_specs=[
            pl.BlockSpec((tm, tn), lambda j, i: (i, j)),
            pl.BlockSpec((1, 2, 2, tn), lambda j, i: (i // tiles_per_seq, 0, 0, j)),
        ],
        out_shape=[jax.ShapeDtypeStruct((m, f), BF16),
                   jax.ShapeDtypeStruct((batch, 2, 2, f), F32)],
        scratch_shapes=[pltpu.VMEM((k, tn), BF16), pltpu.VMEM((k, tn), BF16),
                        pltpu.VMEM((tm + SUBLANES, tn), F32), pltpu.VMEM((tm + SUBLANES, tn), F32)],
        compiler_params=_params(planned, 2),
        name="up_conv_prompt",
    )(xn, w_up, w_up, w_conv, w_conv, b_conv, b_conv)
    return act, st.transpose(0, 2, 1, 3).reshape(batch, 2, 2 * f)


def _up_sample_kernel(x_ref, wa_ref, wg_ref, wca_ref, wcg_ref, bca_ref, bcg_ref, sa_ref, sg_ref,
                      act_ref, u_ref):
    x = x_ref[...]
    ua = jnp.dot(x, wa_ref[...].astype(BF16), preferred_element_type=F32)
    ug = jnp.dot(x, wg_ref[...].astype(BF16), preferred_element_type=F32)
    act = _conv_gate((sa_ref[0], sa_ref[1], ua), (sg_ref[0], sg_ref[1], ug),
                     wca_ref[...], wcg_ref[...], bca_ref[...], bcg_ref[...])
    act_ref[...] = act.astype(act_ref.dtype)
    u_ref[0] = ua
    u_ref[1] = ug


def _up_sample(xn, w_up, w_conv, b_conv, layer, conv_rows):
    n_seq, k = xn.shape
    f = w_up.shape[2] // 2
    tn = _pick(f, (256, 128))
    g0 = f // tn
    planned = 2 * 2 * k * tn * 4 + 2 * k * tn * 2 + 2 * n_seq * k * 2
    act, u = pl.pallas_call(
        _up_sample_kernel,
        grid=(f // tn,),
        in_specs=[
            pl.BlockSpec((n_seq, k), lambda j: (0, 0)),
            pl.BlockSpec((None, k, tn), lambda j: (layer, 0, j)),
            pl.BlockSpec((None, k, tn), lambda j: (layer, 0, j + g0)),
            pl.BlockSpec((None, 3, tn), lambda j: (layer, 0, j)),
            pl.BlockSpec((None, 3, tn), lambda j: (layer, 0, j + g0)),
            pl.BlockSpec((None, 1, tn), lambda j: (layer, 0, j)),
            pl.BlockSpec((None, 1, tn), lambda j: (layer, 0, j + g0)),
            pl.BlockSpec((None, 2, n_seq, tn), lambda j: (layer, 0, 0, j)),
            pl.BlockSpec((None, 2, n_seq, tn), lambda j: (layer, 0, 0, j + g0)),
        ],
        out_specs=[
            pl.BlockSpec((n_seq, tn), lambda j: (0, j)),
            pl.BlockSpec((2, n_seq, tn), lambda j: (0, 0, j)),
        ],
        out_shape=[jax.ShapeDtypeStruct((n_seq, f), BF16),
                   jax.ShapeDtypeStruct((2, n_seq, f), F32)],
        compiler_params=_params(planned, 1),
        name="up_conv_sample",
    )(xn, w_up, w_up, w_conv, w_conv, b_conv, b_conv, conv_rows, conv_rows)
    return act, u.transpose(1, 0, 2).reshape(n_seq, 2 * f)


def _moba_prompt_kernel(q_ref, k_ref, v_ref, o_ref, kb_ref, vt_ref, mean_ref, sel_ref, s_ref,
                        acc_ref, *, n_blocks, scale):
    i = pl.program_id(2)
    blk = q_ref.shape[0]
    hd = q_ref.shape[1]

    @pl.when(i == 0)
    def _():
        mean_ref[...] = jnp.zeros_like(mean_ref)
        for j in range(n_blocks):
            kf = k_ref[j * blk:(j + 1) * blk, :]
            kb_ref[j] = kf.astype(BF16)
            mean_ref[j:j + 1, :] = jnp.mean(kf, axis=0, keepdims=True)
            vt_ref[j, 0:hd, :] = v_ref[j * blk:(j + 1) * blk, :].T.astype(BF16)
            vt_ref[j, hd:2 * hd, :] = jnp.ones((hd, blk), BF16)

    qf = q_ref[...]
    gate_t = _nt(mean_ref[...], qf, precision=HIGHEST)
    blk_id = lax.broadcasted_iota(jnp.int32, gate_t.shape, 0)
    past = blk_id < i
    for j in range(n_blocks):
        gj = gate_t[j:j + 1, :]
        ahead = jnp.where(past & ((gate_t > gj) | ((gate_t == gj) & (blk_id < j))), 1.0, 0.0)
        chosen = jnp.sum(ahead, axis=0, keepdims=True) < MOBA_TOPK
        sel_ref[j] = jnp.broadcast_to(jnp.where(chosen, 1.0, 0.0), (SUBLANES, blk))

    qs = (qf * scale).astype(BF16)
    key_l = lax.broadcasted_iota(jnp.int32, (blk, blk), 0)
    qry_l = lax.broadcasted_iota(jnp.int32, (blk, blk), 1)
    s_own = jnp.where(key_l <= qry_l, _nt(kb_ref[i], qs), NEG_BIG)

    def past_scores(j, mx):
        s = jnp.where(sel_ref[j][0:1, :] > 0.5, _nt(kb_ref[j], qs), NEG_BIG)
        s_ref[j] = s
        return jnp.maximum(mx, s.max(axis=0, keepdims=True))

    mx = lax.fori_loop(0, i, past_scores, s_own.max(axis=0, keepdims=True))
    acc_ref[...] = jnp.dot(vt_ref[i], jnp.exp(s_own - mx).astype(BF16), preferred_element_type=F32)

    def past_values(j, carry):
        p = jnp.exp(s_ref[j] - mx).astype(BF16)
        acc_ref[...] += jnp.dot(vt_ref[j], p, preferred_element_type=F32)
        return carry

    lax.fori_loop(0, i, past_values, 0)
    o_t = acc_ref[0:hd, :] / acc_ref[hd:hd + 1, :]
    o_ref[...] = o_t.T.astype(o_ref.dtype)


def _moba_prompt(q, k, v, batch, seq, heads, hd):
    assert seq % MOBA_BLOCK == 0 and hd == LANES
    n_blocks = seq // MOBA_BLOCK
    mean_rows = -(-n_blocks // SUBLANES) * SUBLANES
    planned = (2 * 2 * seq * hd * 4 + seq * hd * 2 + 2 * hd * seq * 2 + seq * MOBA_BLOCK * 4
               + n_blocks * SUBLANES * MOBA_BLOCK * 4 + 8 * MOBA_BLOCK * MOBA_BLOCK * 4)
    return pl.pallas_call(
        functools.partial(_moba_prompt_kernel, n_blocks=n_blocks, scale=hd ** -0.5),
        grid=(batch, heads, n_blocks),
        in_specs=[
            pl.BlockSpec((MOBA_BLOCK, hd), lambda b, h, i: (b * n_blocks + i, h)),
            pl.BlockSpec((seq, hd), lambda b, h, i: (b, h)),
            pl.BlockSpec((seq, hd), lambda b, h, i: (b, h)),
        ],
        out_specs=pl.BlockSpec((MOBA_BLOCK, hd), lambda b, h, i: (b * n_blocks + i, h)),
        out_shape=jax.ShapeDtypeStruct((batch * seq, heads * hd), BF16),
        scratch_shapes=[pltpu.VMEM((n_blocks, MOBA_BLOCK, hd), BF16),
                        pltpu.VMEM((n_blocks, 2 * hd, MOBA_BLOCK), BF16),
                        pltpu.VMEM((mean_rows, hd), F32),
                        pltpu.VMEM((n_blocks, SUBLANES, MOBA_BLOCK), F32),
                        pltpu.VMEM((n_blocks, MOBA_BLOCK, MOBA_BLOCK), F32),
                        pltpu.VMEM((2 * hd, MOBA_BLOCK), F32)],
        compiler_params=_params(planned, 3),
        name="moba_prompt",
    )(q, k, v)


def _moba_select_kernel(pt_ref, *refs, pages_per_block, n_blocks, n_top):
    ck_refs = refs[:pages_per_block]
    q_ref, idx_ref, mean_ref = refs[pages_per_block:]
    blk = pl.program_id(1)
    page, heads, hd = ck_refs[0].shape[1:]
    total = jnp.sum(ck_refs[0][0], axis=0)
    for ck_ref in ck_refs[1:]:
        total = total + jnp.sum(ck_ref[0], axis=0)
    mean_ref[blk] = total * (1.0 / (pages_per_block * page))

    @pl.when(blk == n_blocks - 1)
    def _():
        prod = (mean_ref[...] * q_ref[...]).reshape(n_blocks * heads, hd)
        gate = jnp.dot(prod, jnp.ones((hd, LANES), F32), precision=HIGHEST,
                       preferred_element_type=F32).reshape(n_blocks, heads, LANES)
        blk_id = lax.broadcasted_iota(jnp.int32, gate.shape, 0)
        for t in range(n_top):
            best = gate.max(axis=0)
            first = jnp.where(gate == best[None], blk_id, n_blocks).min(axis=0)
            idx_ref[0, t] = first
            gate = jnp.where(blk_id == first[None], -jnp.inf, gate)


def _moba_select(page_table, cache_k, q):
    n_seq, n_pages = page_table.shape
    _, page, heads, hd = cache_k.shape
    pages_per_block = MOBA_BLOCK // page
    assert n_pages % pages_per_block == 0 and heads % SUBLANES == 0
    n_blocks = n_pages // pages_per_block
    n_top = min(MOBA_TOPK, n_blocks)

    def page_spec(pp):
        return pl.BlockSpec((1, page, heads, hd),
                            lambda b, n, pt: (pt[b, n * pages_per_block + pp], 0, 0, 0))

    planned = (2 * pages_per_block * page * heads * hd * 4 + 4 * n_blocks * heads * hd * 4
               + 2 * n_top * heads * LANES * 4)
    grid_spec = pltpu.PrefetchScalarGridSpec(
        num_scalar_prefetch=1,
        grid=(n_seq, n_blocks),
        in_specs=[page_spec(pp) for pp in range(pages_per_block)]
        + [pl.BlockSpec((1, heads, hd), lambda b, n, pt: (b, 0, 0))],
        out_specs=pl.BlockSpec((1, n_top, heads, LANES), lambda b, n, pt: (b, 0, 0, 0)),
        scratch_shapes=[pltpu.VMEM((n_blocks, heads, hd), F32)],
    )
    idx = pl.pallas_call(
        functools.partial(_moba_select_kernel, pages_per_block=pages_per_block, n_blocks=n_blocks,
                          n_top=n_top),
        grid_spec=grid_spec,
        out_shape=jax.ShapeDtypeStruct((n_seq, n_top, heads, LANES), jnp.int32),
        compiler_params=_params(planned, 2),
        name="moba_select",
    )(page_table, *([cache_k] * pages_per_block), q)
    return idx[..., 0]


def _moba_sample_kernel(pt_ref, idx_ref, *refs, n_sel, scale):
    k_refs = refs[:n_sel]
    v_refs = refs[n_sel:2 * n_sel]
    q_ref, kn_ref, vn_ref, o_ref = refs[2 * n_sel:]
    page, group, hd = k_refs[0].shape[1:]
    rows = page * group
    mine = lax.broadcasted_iota(jnp.int32, (SUBLANES, rows), 1) % group == pl.program_id(1) % group
    q8 = jnp.broadcast_to(q_ref[0] * scale, (SUBLANES, hd))
    q8b = q8.astype(BF16)
    scores = [jnp.where(mine, _nt(q8b, k_ref[0].reshape(rows, hd).astype(BF16)), NEG_BIG)
              for k_ref in k_refs]
    s_own = jnp.sum(q8 * kn_ref[0], axis=-1, keepdims=True)
    mx = s_own
    for s in scores:
        mx = jnp.maximum(mx, s.max(axis=-1, keepdims=True))
    p_own = jnp.exp(s_own - mx)
    denom = p_own
    acc = p_own * vn_ref[0]
    for s, v_ref in zip(scores, v_refs):
        p = jnp.exp(s - mx)
        denom = denom + jnp.sum(p, axis=-1, keepdims=True)
        acc = acc + jnp.dot(p.astype(BF16), v_ref[0].reshape(rows, hd).astype(BF16),
                            preferred_element_type=F32)
    o_ref[0] = (acc / denom)[0:1].astype(o_ref.dtype)


def _moba_sample(page_table, picks, cache_k, cache_v, q, k_new, v_new):
    n_seq, n_pages = page_table.shape
    _, page, heads, hd = cache_k.shape
    pages_per_block = MOBA_BLOCK // page
    n_top = picks.shape[1]
    n_sel = n_top * pages_per_block

    def page_spec(t, pp):
        return pl.BlockSpec(
            (1, page, SUBLANES, hd),
            lambda b, h, pt, ix: (pt[b, ix[b, t, h] * pages_per_block + pp], 0, h // SUBLANES, 0))

    page_specs = [page_spec(t, pp) for t in range(n_top) for pp in range(pages_per_block)]
    vec_spec = pl.BlockSpec((1, 1, hd), lambda b, h, pt, ix: (b, 0, h))
    planned = 2 * 2 * n_sel * page * SUBLANES * hd * 4 + 16 * page * SUBLANES * hd * 4
    grid_spec = pltpu.PrefetchScalarGridSpec(
        num_scalar_prefetch=2,
        grid=(n_seq, heads),
        in_specs=page_specs + page_specs + [vec_spec, vec_spec, vec_spec],
        out_specs=vec_spec,
    )
    return pl.pallas_call(
        functools.partial(_moba_sample_kernel, n_sel=n_sel, scale=hd ** -0.5),
        grid_spec=grid_spec,
        out_shape=jax.ShapeDtypeStruct((n_seq, 1, heads * hd), BF16),
        compiler_params=_params(planned, 2),
        name="moba_sample",
    )(page_table, picks, *([cache_k] * n_sel), *([cache_v] * n_sel), q, k_new, v_new)


def _rope_tables(pos, hd):
    half = hd // 2
    inv = ROPE_THETA ** (-jnp.arange(half, dtype=F32) / half)
    ang = pos[:, None] * inv[None, :]
    cos, sin = jnp.cos(ang), jnp.sin(ang)
    return jnp.concatenate([cos, cos], axis=-1), jnp.concatenate([-sin, sin], axis=-1)


def kernel(x_prompt, x_sample, cache_k, cache_v, page_table, state_gla, state_conv, g_mix, w_in_a,
           w_a2, b_a2, g_gla_head, w_out_a, g_kv, w_kv, w_q, w_out_b, g_ffn, w_up, w_conv, b_conv,
           w_down, g_final):
    n_b, seq, d = x_prompt.shape
    n_s, t_s, _ = x_sample.shape
    assert t_s == 1
    _, _, gla_h, dk, dv = state_gla.shape
    _, page, att_h, hd = cache_k.shape
    n_pages = page_table.shape[1]
    past_len = n_pages * page
    rank = w_a2.shape[1]
    qk_cols, v_cols = gla_h * dk, gla_h * dv
    proj_cols = 2 * qk_cols + 2 * v_cols
    da = att_h * hd
    assert rank <= LANES and len(w_in_a) == 1 and len(w_q) == 1

    xp = x_prompt.reshape(n_b * seq, d)
    xs = x_sample.reshape(n_s, d)
    rope_p = _rope_tables(jnp.arange(seq, dtype=F32), hd)
    rope_s = tuple(jnp.tile(t, (n_s, 1)) for t in _rope_tables(past_len + jnp.arange(t_s, dtype=F32), hd))

    w_a1_pad = jnp.pad(w_in_a[:, :, proj_cols:], ((0, 0), (0, 0), (0, LANES - rank)))
    w_a2_pad = jnp.pad(w_a2[0], ((0, LANES - rank), (0, 0)))
    conv_rows = state_conv.transpose(0, 2, 1, 3)
    b_conv3 = b_conv[:, None, :]
    down_split = 2 if (w_down.shape[1] // 2) % LANES == 0 else 1

    def layer0(h, prompt):
        (hn,) = _rmsnorm(h, g_mix[0:1], BF16)
        proj = _matmul(hn, w_in_a, 0, proj_cols)
        a_pad = _matmul(hn, w_a1_pad, 0, LANES)
        if prompt:
            og, s = _gla_prompt(proj, a_pad, w_a2_pad, b_a2[0:1], g_gla_head[0:1], n_b, seq, gla_h, dk, dv)
        else:
            og, s = _gla_sample(proj.reshape(n_s, 1, proj_cols), a_pad.reshape(n_s, 1, LANES), w_a2_pad,
                                b_a2[0:1], g_gla_head[0:1], state_gla, 0, gla_h, dk, dv)
            og = og.reshape(n_s, v_cols)
        return _matmul(og, w_out_a, 0, d, res=h), s

    def conv_ffn(h, layer, prompt):
        (hn,) = _rmsnorm(h, g_ffn[layer:layer + 1], BF16)
        if prompt:
            act, st = _up_prompt(hn, w_up, w_conv, b_conv3, layer, n_b, seq)
        else:
            act, u = _up_sample(hn, w_up, w_conv, b_conv3, layer, conv_rows)
            st = jnp.stack([state_conv[layer][:, 1, :], u], axis=1)
        return _matmul(act, w_down, layer, d, k_split=down_split, res=h), st

    def layer1(h, prompt):
        hn, kvn = _rmsnorm(h, jnp.stack([g_mix[1], g_kv]), BF16)
        rope = rope_p if prompt else rope_s
        k = _matmul(kvn, w_kv[None], 0, da, rope=rope)
        v = _matmul(kvn, w_kv[None], 0, da, w_col0=da)
        q = _matmul(hn, w_q, 0, da, rope=rope)
        if prompt:
            att = _moba_prompt(q, k, v, n_b, seq, att_h, hd)
        else:
            q3, k3, v3 = (t.reshape(n_s, 1, da) for t in (q, k, v))
            picks = _moba_select(page_table, cache_k, q.reshape(n_s, att_h, hd))
            att = _moba_sample(page_table, picks, cache_k, cache_v, q3, k3, v3)
            att = att.reshape(n_s, da)
        return _matmul(att, w_out_b, 0, d, res=h), k, v

    def trunk(h, prompt):
        h, s_gla = layer0(h, prompt)
        h, st0 = conv_ffn(h, 0, prompt)
        h, k, v = layer1(h, prompt)
        h, st1 = conv_ffn(h, 1, prompt)
        (y,) = _rmsnorm(h, g_final[None, :], F32)
        return y, k, v, s_gla[None], jnp.stack([st0, st1])

    y_p, k_p, v_p, gla_p, conv_p = trunk(xp, True)
    y_s, k_s, v_s, gla_s, conv_s = trunk(xs, False)
    return (y_p.reshape(n_b, seq, d), y_s.reshape(n_s, t_s, d),
            k_p.reshape(n_b, seq, att_h, hd), v_p.reshape(n_b, seq, att_h, hd), gla_p, conv_p,
            k_s.reshape(n_s, t_s, att_h, hd), v_s.reshape(n_s, t_s, att_h, hd), gla_s, conv_s)
```

```python
import functools

import jax
import jax.numpy as jnp
from jax import lax
from jax.experimental import pallas as pl
from jax.experimental.pallas import tpu as pltpu

F32 = jnp.float32
BF16 = jnp.bfloat16
HIGHEST = lax.Precision.HIGHEST

GLA_GATE_TEMP = 16.0
GLA_CHUNK = 64
MOBA_BLOCK = 256
MOBA_TOPK = 3
ROPE_THETA = 10000.0
EPS = 1e-6

LANES = 128
SUBLANES = 8
VMEM_PHYSICAL_BYTES = 64 * 1024 * 1024
VMEM_BUDGET_BYTES = 44 * 1024 * 1024
VMEM_COMPILER_SLACK_BYTES = 8 * 1024 * 1024
NEG_BIG = -1e30


def _vmem_limit(planned_bytes):
    return int(min(planned_bytes + VMEM_COMPILER_SLACK_BYTES, VMEM_PHYSICAL_BYTES - 4 * 1024 * 1024))


def _params(planned_bytes, n_axes):
    return pltpu.CompilerParams(
        dimension_semantics=("arbitrary",) * n_axes,
        vmem_limit_bytes=_vmem_limit(planned_bytes),
    )


def _nt(a, b, precision=None):
    return lax.dot_general(a, b, (((1,), (1,)), ((), ())), precision=precision,
                           preferred_element_type=F32)


def _tn(a, b, precision=None):
    return lax.dot_general(a, b, (((0,), (0,)), ((), ())), precision=precision,
                           preferred_element_type=F32)


def _sigmoid(x):
    return 1.0 / (1.0 + jnp.exp(-x))


def _log_sigmoid(z):
    return jnp.minimum(z, 0.0) - jnp.log1p(jnp.exp(-jnp.abs(z)))


def _pick(n, candidates):
    for c in candidates:
        if n % c == 0:
            return c
    return n


def _norm_kernel(x_ref, g_ref, *o_refs):
    x = x_ref[...]
    y = x * lax.rsqrt(jnp.mean(x * x, axis=-1, keepdims=True) + EPS)
    for i, o_ref in enumerate(o_refs):
        o_ref[...] = (y * g_ref[i:i + 1, :]).astype(o_ref.dtype)


def _rmsnorm(x, gains, out_dtype):
    m, d = x.shape
    n_g = gains.shape[0]
    tm = _pick(m, (256, 128, 64, 32, 16, 8))
    planned = 2 * tm * d * 4 * (1 + n_g)
    outs = pl.pallas_call(
        _norm_kernel,
        grid=(m // tm,),
        in_specs=[pl.BlockSpec((tm, d), lambda i: (i, 0)),
                  pl.BlockSpec((n_g, d), lambda i: (0, 0))],
        out_specs=[pl.BlockSpec((tm, d), lambda i: (i, 0)) for _ in range(n_g)],
        out_shape=[jax.ShapeDtypeStruct((m, d), out_dtype) for _ in range(n_g)],
        compiler_params=_params(planned, 1),
        name="rmsnorm",
    )(x, gains)
    return outs


def _rope_tile(acc, cos, sin_signed):
    heads = []
    for c in range(acc.shape[1] // LANES):
        xc = acc[:, c * LANES:(c + 1) * LANES]
        heads.append(xc * cos + pltpu.roll(xc, LANES // 2, 1) * sin_signed)
    return heads[0] if len(heads) == 1 else jnp.concatenate(heads, axis=1)


def _cast_weight_tile(w_ref, wb_ref):
    k = w_ref.shape[0]
    ck = _pick(k, (512, 256, 128))

    def body(i, carry):
        r = pl.multiple_of(i * ck, ck)
        wb_ref[pl.ds(r, ck), :] = w_ref[pl.ds(r, ck), :].astype(BF16)
        return carry

    lax.fori_loop(0, k // ck, body, 0)


def _mm_kernel(*refs, has_res, has_rope):
    x_ref, xs_ref, w_ref = refs[:3]
    pos = 3
    res_refs = rope_refs = None
    if has_res:
        res_refs = refs[pos:pos + 2]
        pos += 2
    if has_rope:
        rope_refs = refs[pos:pos + 4]
        pos += 4
    o_ref, os_ref, wb_ref = refs[pos:pos + 3]

    def project(lhs_ref, group):
        acc = jnp.dot(lhs_ref[...], wb_ref[...], preferred_element_type=F32)
        if has_rope:
            acc = _rope_tile(acc, rope_refs[2 * group][...], rope_refs[2 * group + 1][...])
        if has_res:
            acc = acc + res_refs[group][...]
        return acc

    @pl.when(pl.program_id(1) == 0)
    def _():
        _cast_weight_tile(w_ref, wb_ref)
        os_ref[...] = project(xs_ref, 1).astype(os_ref.dtype)

    o_ref[...] = project(x_ref, 0).astype(o_ref.dtype)


def _mm_tiles(m, k, n, has_res, row_period):
    for tm, tn in ((1024, 512), (512, 512), (512, 256), (256, 256), (256, 128), (128, 128),
                   (64, 128), (32, 128), (16, 128), (8, 128)):
        tm = min(tm, m)
        if m % tm or n % tn or row_period % tm:
            continue
        planned = (2 * k * tn * 4 + k * tn * 2 + 2 * tm * k * 2
                   + (2 + 2 * has_res + 1) * tm * tn * 4)
        if planned <= VMEM_BUDGET_BYTES:
            return tm, tn, planned
    raise ValueError(f"no matmul tiling for {(m, k, n)}")


def _matmul(x, xs, w, layer, n_out, *, w_col0=0, k_split=1, res=None, rope=None, out_dtype=F32):
    if k_split > 1:
        assert rope is None
        for part in range(k_split):
            res = _matmul_part(x, xs, w, layer, n_out, w_col0, k_split, part, res, None,
                               out_dtype if part == k_split - 1 else F32)
        return res
    return _matmul_part(x, xs, w, layer, n_out, w_col0, 1, 0, res, rope, out_dtype)


def _matmul_part(x, xs, w, layer, n_out, w_col0, k_split, part, res, rope, out_dtype):
    m, ms = x.shape[0], xs.shape[0]
    k = x.shape[1] // k_split
    assert x.shape[1] % k_split == 0 and k % LANES == 0
    tm, tn, planned = _mm_tiles(m, k, n_out, res is not None, m if rope is None else rope[0][0].shape[0])
    assert w_col0 % tn == 0
    col0 = w_col0 // tn
    row_tile = pl.BlockSpec((tm, tn), lambda j, i: (i, j))
    sample_tile = pl.BlockSpec((ms, tn), lambda j, i: (0, j))
    in_specs = [pl.BlockSpec((tm, k), lambda j, i: (i, part)),
                pl.BlockSpec((ms, k), lambda j, i: (0, part)),
                pl.BlockSpec((None, k, tn), lambda j, i: (layer, part, j + col0))]
    args = [x, xs, w]
    if res is not None:
        in_specs += [row_tile, sample_tile]
        args += list(res)
    if rope is not None:
        (cos, sin_signed), (cos_s, sin_s) = rope
        p_tiles = cos.shape[0] // tm
        assert cos.shape[0] % tm == 0 and tn % LANES == 0 and cos_s.shape[0] == ms
        in_specs += [pl.BlockSpec((tm, LANES), lambda j, i: (i % p_tiles, 0))] * 2
        in_specs += [pl.BlockSpec((ms, LANES), lambda j, i: (0, 0))] * 2
        args += [cos, sin_signed, cos_s, sin_s]
    return pl.pallas_call(
        functools.partial(_mm_kernel, has_res=res is not None, has_rope=rope is not None),
        grid=(n_out // tn, m // tm),
        in_specs=in_specs,
        out_specs=[row_tile, sample_tile],
        out_shape=[jax.ShapeDtypeStruct((m, n_out), out_dtype),
                   jax.ShapeDtypeStruct((ms, n_out), out_dtype)],
        scratch_shapes=[pltpu.VMEM((k, tn), BF16)],
        compiler_params=_params(planned, 2),
        name="matmul",
    )(*args)


def _gla_gate_norm(o, r, gh):
    y = o * lax.rsqrt(jnp.mean(o * o, axis=-1, keepdims=True) + EPS) * gh
    return y * (r * _sigmoid(r))


def _gla_prompt_kernel(q_ref, k_ref, v_ref, r_ref, a_ref, wa_ref, ba_ref, gh_ref,
                       o_ref, s_ref, *, dk):
    c = pl.program_id(2)

    @pl.when(c == 0)
    def _():
        s_ref[...] = jnp.zeros_like(s_ref)

    n_c = q_ref.shape[0]
    dv = v_ref.shape[1]
    z = jnp.dot(a_ref[...], wa_ref[...], precision=HIGHEST, preferred_element_type=F32) + ba_ref[...]
    g = _log_sigmoid(z) / GLA_GATE_TEMP
    row = lax.broadcasted_iota(jnp.int32, (n_c, n_c), 0)
    col = lax.broadcasted_iota(jnp.int32, (n_c, n_c), 1)
    causal = col <= row
    b = jnp.dot(causal.astype(F32), g, precision=HIGHEST, preferred_element_type=F32)
    b_last = b[n_c - 1:n_c, :]
    b_last_col = _tn(g, jnp.ones((n_c, LANES), F32), precision=HIGHEST)

    q = q_ref[...] * (dk ** -0.5)
    k = k_ref[...]
    v = v_ref[...].astype(BF16)
    qe = (q * jnp.exp(b)).astype(BF16)
    kd = (k * jnp.exp(-b)).astype(BF16)
    kr = (k * jnp.exp(b_last - b)).astype(BF16)
    att = jnp.where(causal, _nt(qe, kd), 0.0).astype(BF16)
    s_old = s_ref[0, 0]
    o = (jnp.dot(att, v, preferred_element_type=F32)
         + jnp.dot(qe, s_old.astype(BF16), preferred_element_type=F32))
    upd = _tn(kr, v)
    decay = jnp.exp(b_last_col)
    for t in range(dv // LANES):
        sl = slice(t * LANES, (t + 1) * LANES)
        s_ref[0, 0, :, sl] = s_old[:, sl] * decay + upd[:, sl]
    o_ref[...] = _gla_gate_norm(o, r_ref[...], gh_ref[...]).astype(o_ref.dtype)


def _gla_prompt(proj, a_pad, wa_pad, b_a2, g_head, batch, seq, heads, dk, dv):
    assert seq % GLA_CHUNK == 0 and dv == 2 * dk
    n_chunks = seq // GLA_CHUNK
    m = batch * seq
    rows = lambda b, h, c: b * n_chunks + c
    planned = 2 * (3 * GLA_CHUNK * dk + 3 * GLA_CHUNK * dv + 2 * dk * dv) * 4 + 6 * dk * dv * 4
    return pl.pallas_call(
        functools.partial(_gla_prompt_kernel, dk=dk),
        grid=(batch, heads, n_chunks),
        in_specs=[
            pl.BlockSpec((GLA_CHUNK, dk), lambda b, h, c: (rows(b, h, c), h)),
            pl.BlockSpec((GLA_CHUNK, dk), lambda b, h, c: (rows(b, h, c), heads + h)),
            pl.BlockSpec((GLA_CHUNK, dv), lambda b, h, c: (rows(b, h, c), heads + h)),
            pl.BlockSpec((GLA_CHUNK, dv), lambda b, h, c: (rows(b, h, c), 2 * heads + h)),
            pl.BlockSpec((GLA_CHUNK, LANES), lambda b, h, c: (rows(b, h, c), 0)),
            pl.BlockSpec((LANES, dk), lambda b, h, c: (0, h)),
            pl.BlockSpec((1, dk), lambda b, h, c: (0, h)),
            pl.BlockSpec((1, dv), lambda b, h, c: (0, 0)),
        ],
        out_specs=[
            pl.BlockSpec((GLA_CHUNK, dv), lambda b, h, c: (rows(b, h, c), h)),
            pl.BlockSpec((1, 1, dk, dv), lambda b, h, c: (b, h, 0, 0)),
        ],
        out_shape=[jax.ShapeDtypeStruct((m, heads * dv), BF16),
                   jax.ShapeDtypeStruct((batch, heads, dk, dv), F32)],
        compiler_params=_params(planned, 3),
        name="gla_prompt",
    )(proj, proj, proj, proj, a_pad, wa_pad, b_a2, g_head)


def _gla_sample_kernel(q_ref, k_ref, v_ref, r_ref, a_ref, wa_ref, ba_ref, gh_ref, s0_ref,
                       o_ref, s_ref, *, dk):
    a8 = jnp.broadcast_to(a_ref[0], (SUBLANES, LANES))
    z = jnp.dot(a8, wa_ref[...], precision=HIGHEST, preferred_element_type=F32)[0:1] + ba_ref[...]
    g = _log_sigmoid(z) / GLA_GATE_TEMP
    q = q_ref[0] * (dk ** -0.5)
    k = k_ref[0]
    stacked = jnp.concatenate([q, k, g, jnp.zeros((SUBLANES - 3, q.shape[1]), F32)], axis=0)
    cols = stacked.T
    qc, kc, gc = cols[:, 0:1], cols[:, 1:2], cols[:, 2:3]
    s_new = jnp.exp(gc) * s0_ref[0, 0, 0] + kc * v_ref[0]
    s_ref[0, 0] = s_new
    o = jnp.sum(qc * s_new, axis=0, keepdims=True)
    o_ref[0] = _gla_gate_norm(o, r_ref[0], gh_ref[...]).astype(o_ref.dtype)


def _gla_sample(proj, a_pad, wa_pad, b_a2, g_head, state, layer, heads, dk, dv):
    n_seq = proj.shape[0]
    planned = 2 * 2 * dk * dv * 4 + 4 * dk * dv * 4
    return pl.pallas_call(
        functools.partial(_gla_sample_kernel, dk=dk),
        grid=(n_seq, heads),
        in_specs=[
            pl.BlockSpec((1, 1, dk), lambda b, h: (b, 0, h)),
            pl.BlockSpec((1, 1, dk), lambda b, h: (b, 0, heads + h)),
            pl.BlockSpec((1, 1, dv), lambda b, h: (b, 0, heads + h)),
            pl.BlockSpec((1, 1, dv), lambda b, h: (b, 0, 2 * heads + h)),
            pl.BlockSpec((1, 1, LANES), lambda b, h: (b, 0, 0)),
            pl.BlockSpec((LANES, dk), lambda b, h: (0, h)),
            pl.BlockSpec((1, dk), lambda b, h: (0, h)),
            pl.BlockSpec((1, dv), lambda b, h: (0, 0)),
            pl.BlockSpec((1, 1, 1, dk, dv), lambda b, h: (layer, b, h, 0, 0)),
        ],
        out_specs=[
            pl.BlockSpec((1, 1, dv), lambda b, h: (b, 0, h)),
            pl.BlockSpec((1, 1, dk, dv), lambda b, h: (b, h, 0, 0)),
        ],
        out_shape=[jax.ShapeDtypeStruct((n_seq, 1, heads * dv), BF16),
                   jax.ShapeDtypeStruct((n_seq, heads, dk, dv), F32)],
        compiler_params=_params(planned, 2),
        name="gla_sample",
    )(proj, proj, proj, proj, a_pad, wa_pad, b_a2, g_head, state)


def _conv_gate(ua, ug, wca, wcg, bca, bcg):
    ca = bca + wca[0:1] * ua[0] + wca[1:2] * ua[1] + wca[2:3] * ua[2]
    cg = bcg + wcg[0:1] * ug[0] + wcg[1:2] * ug[1] + wcg[2:3] * ug[2]
    return ca * _sigmoid(ca) * cg


def _up_kernel(x_ref, xs_ref, wa_ref, wg_ref, wca_ref, wcg_ref, bca_ref, bcg_ref, sa_ref, sg_ref,
               act_ref, st_ref, acts_ref, us_ref, wba_ref, wbg_ref, ua_ref, ug_ref, *, tiles_per_seq):
    i = pl.program_id(1)
    tm = x_ref.shape[0]
    conv = lambda ua, ug: _conv_gate(ua, ug, wca_ref[...], wcg_ref[...], bca_ref[...], bcg_ref[...])

    @pl.when(i == 0)
    def _():
        _cast_weight_tile(wa_ref, wba_ref)
        _cast_weight_tile(wg_ref, wbg_ref)
        xs = xs_ref[...]
        ua = jnp.dot(xs, wba_ref[...], preferred_element_type=F32)
        ug = jnp.dot(xs, wbg_ref[...], preferred_element_type=F32)
        acts_ref[...] = conv((sa_ref[0], sa_ref[1], ua), (sg_ref[0], sg_ref[1], ug)).astype(acts_ref.dtype)
        us_ref[0] = ua
        us_ref[1] = ug

    @pl.when(i % tiles_per_seq == 0)
    def _():
        ua_ref[0:SUBLANES, :] = jnp.zeros((SUBLANES, ua_ref.shape[1]), F32)
        ug_ref[0:SUBLANES, :] = jnp.zeros((SUBLANES, ug_ref.shape[1]), F32)

    x = x_ref[...]
    ua_ref[SUBLANES:, :] = jnp.dot(x, wba_ref[...], preferred_element_type=F32)
    ug_ref[SUBLANES:, :] = jnp.dot(x, wbg_ref[...], preferred_element_type=F32)
    taps = lambda u_ref: tuple(u_ref[pl.ds(SUBLANES - 2 + d, tm), :] for d in range(3))
    act_ref[...] = conv(taps(ua_ref), taps(ug_ref)).astype(act_ref.dtype)
    st_ref[0, 0] = ua_ref[pl.ds(tm + SUBLANES - 2, 2), :]
    st_ref[0, 1] = ug_ref[pl.ds(tm + SUBLANES - 2, 2), :]
    ua_ref[0:SUBLANES, :] = ua_ref[pl.ds(tm, SUBLANES), :]
    ug_ref[0:SUBLANES, :] = ug_ref[pl.ds(tm, SUBLANES), :]


def _up_tiles(m, k, f, seq):
    for tm, tn in ((1024, 256), (512, 256), (512, 128), (256, 128), (128, 128)):
        if seq % tm or f % tn:
            continue
        planned = (2 * 2 * k * tn * 4 + 2 * k * tn * 2 + 2 * tm * k * 2
                   + 2 * (tm + SUBLANES) * tn * 4 + 2 * tm * tn * 2 + 4 * tm * tn * 4)
        if planned <= VMEM_BUDGET_BYTES:
            return tm, tn, planned
    raise ValueError(f"no up-projection tiling for {(m, k, f)}")


def _up(xn, xs, w_up, w_conv, b_conv, conv_rows, layer, batch, seq):
    m, k = xn.shape
    ms = xs.shape[0]
    f = w_up.shape[2] // 2
    assert w_conv.shape[1] == 3
    tm, tn, planned = _up_tiles(m, k, f, seq)
    g0 = f // tn
    tiles_per_seq = seq // tm
    half = lambda block, off: pl.BlockSpec((None,) + block, lambda j, i: (layer,) + (0,) * (len(block) - 1) + (j + off,))
    act, st, act_s, u_s = pl.pallas_call(
        functools.partial(_up_kernel, tiles_per_seq=tiles_per_seq),
        grid=(f // tn, m // tm),
        in_specs=[
            pl.BlockSpec((tm, k), lambda j, i: (i, 0)),
            pl.BlockSpec((ms, k), lambda j, i: (0, 0)),
            half((k, tn), 0), half((k, tn), g0),
            half((3, tn), 0), half((3, tn), g0),
            half((1, tn), 0), half((1, tn), g0),
            half((2, ms, tn), 0), half((2, ms, tn), g0),
        ],
        out_specs=[
            pl.BlockSpec((tm, tn), lambda j, i: (i, j)),
            pl.BlockSpec((1, 2, 2, tn), lambda j, i: (i // tiles_per_seq, 0, 0, j)),
            pl.BlockSpec((ms, tn), lambda j, i: (0, j)),
            pl.BlockSpec((2, ms, tn), lambda j, i: (0, 0, j)),
        ],
        out_shape=[jax.ShapeDtypeStruct((m, f), BF16),
                   jax.ShapeDtypeStruct((batch, 2, 2, f), F32),
                   jax.ShapeDtypeStruct((ms, f), BF16),
                   jax.ShapeDtypeStruct((2, ms, f), F32)],
        scratch_shapes=[pltpu.VMEM((k, tn), BF16), pltpu.VMEM((k, tn), BF16),
                        pltpu.VMEM((tm + SUBLANES, tn), F32), pltpu.VMEM((tm + SUBLANES, tn), F32)],
        compiler_params=_params(planned, 2),
        name="up_conv",
    )(xn, xs, w_up, w_up, w_conv, w_conv, b_conv, b_conv, conv_rows, conv_rows)
    return (act, st.transpose(0, 2, 1, 3).reshape(batch, 2, 2 * f),
            act_s, u_s.transpose(1, 0, 2).reshape(ms, 2 * f))


def _moba_prompt_kernel(q_ref, k_ref, v_ref, o_ref, kb_ref, vt_ref, mean_ref, p_ref,
                        *, n_blocks, scale):
    blk = MOBA_BLOCK
    hd = q_ref.shape[1]
    mean_ref[...] = jnp.zeros_like(mean_ref)
    for j in range(n_blocks):
        rows = slice(j * blk, (j + 1) * blk)
        kf = k_ref[rows, :]
        kb_ref[rows, :] = kf.astype(BF16)
        mean_ref[j:j + 1, :] = jnp.mean(kf, axis=0, keepdims=True)
        vt_ref[0:hd, rows] = v_ref[rows, :].T.astype(BF16)
    vt_ref[hd:2 * hd, :] = jnp.ones((hd, vt_ref.shape[1]), BF16)

    key_l = lax.broadcasted_iota(jnp.int32, (blk, blk), 0)
    qry_l = lax.broadcasted_iota(jnp.int32, (blk, blk), 1)
    causal = key_l <= qry_l
    for i in range(n_blocks):
        n_keys = (i + 1) * blk
        qf = q_ref[i * blk:(i + 1) * blk, :]
        s_t = _nt(kb_ref[0:n_keys, :], (qf * scale).astype(BF16))
        parts = [s_t[j * blk:(j + 1) * blk, :] for j in range(i + 1)]
        parts[i] = jnp.where(causal, parts[i], NEG_BIG)
        if i > MOBA_TOPK:
            gate_t = _nt(mean_ref[...], qf, precision=HIGHEST)
            blk_id = lax.broadcasted_iota(jnp.int32, gate_t.shape, 0)
            for j in range(i):
                gj = gate_t[j:j + 1, :]
                beats = (blk_id < i) & ((gate_t > gj) | ((gate_t == gj) & (blk_id < j)))
                ahead = jnp.sum(jnp.where(beats, 1.0, 0.0), axis=0, keepdims=True)
                parts[j] = jnp.where(ahead < MOBA_TOPK, parts[j], NEG_BIG)
        mx = parts[0].max(axis=0, keepdims=True)
        for part in parts[1:]:
            mx = jnp.maximum(mx, part.max(axis=0, keepdims=True))
        for j, part in enumerate(parts):
            p_ref[j * blk:(j + 1) * blk, :] = jnp.exp(part - mx).astype(BF16)
        o_t = jnp.dot(vt_ref[:, 0:n_keys], p_ref[0:n_keys, :], preferred_element_type=F32)
        o_t = o_t[0:hd, :] / o_t[hd:hd + 1, :]
        o_ref[i * blk:(i + 1) * blk, :] = o_t.T.astype(o_ref.dtype)


def _moba_prompt(q, k, v, batch, seq, heads, hd):
    assert seq % MOBA_BLOCK == 0 and hd == LANES
    n_blocks = seq // MOBA_BLOCK
    mean_rows = -(-n_blocks // SUBLANES) * SUBLANES
    planned = (2 * 3 * seq * hd * 4 + 2 * seq * hd * 2 + seq * hd * 2 + 2 * hd * seq * 2
               + seq * MOBA_BLOCK * 2 + 4 * seq * MOBA_BLOCK * 4)
    head_rows = pl.BlockSpec((seq, hd), lambda b, h: (b, h))
    return pl.pallas_call(
        functools.partial(_moba_prompt_kernel, n_blocks=n_blocks, scale=hd ** -0.5),
        grid=(batch, heads),
        in_specs=[head_rows, head_rows, head_rows],
        out_specs=head_rows,
        out_shape=jax.ShapeDtypeStruct((batch * seq, heads * hd), BF16),
        scratch_shapes=[pltpu.VMEM((seq, hd), BF16), pltpu.VMEM((2 * hd, seq), BF16),
                        pltpu.VMEM((mean_rows, hd), F32), pltpu.VMEM((seq, MOBA_BLOCK), BF16)],
        compiler_params=_params(planned, 2),
        name="moba_prompt",
    )(q, k, v)


def _moba_select_kernel(pt_ref, *refs, pages_per_block, n_blocks, n_top):
    ck_refs = refs[:pages_per_block]
    q_ref, idx_ref, mean_ref = refs[pages_per_block:]
    blk = pl.program_id(1)
    page, heads, hd = ck_refs[0].shape[1:]
    total = jnp.sum(ck_refs[0][0], axis=0)
    for ck_ref in ck_refs[1:]:
        total = total + jnp.sum(ck_ref[0], axis=0)
    mean_ref[blk] = total * (1.0 / (pages_per_block * page))

    @pl.when(blk == n_blocks - 1)
    def _():
        prod = (mean_ref[...] * q_ref[...]).reshape(n_blocks * heads, hd)
        gate = jnp.dot(prod, jnp.ones((hd, LANES), F32), precision=HIGHEST,
                       preferred_element_type=F32).reshape(n_blocks, heads, LANES)
        blk_id = lax.broadcasted_iota(jnp.int32, gate.shape, 0)
        for t in range(n_top):
            best = gate.max(axis=0)
            first = jnp.where(gate == best[None], blk_id, n_blocks).min(axis=0)
            idx_ref[0, t] = first
            gate = jnp.where(blk_id == first[None], -jnp.inf, gate)


def _moba_select(page_table, cache_k, q):
    n_seq, n_pages = page_table.shape
    _, page, heads, hd = cache_k.shape
    pages_per_block = MOBA_BLOCK // page
    assert n_pages % pages_per_block == 0 and heads % SUBLANES == 0
    n_blocks = n_pages // pages_per_block
    n_top = min(MOBA_TOPK, n_blocks)

    def page_spec(pp):
        return pl.BlockSpec((1, page, heads, hd),
                            lambda b, n, pt: (pt[b, n * pages_per_block + pp], 0, 0, 0))

    planned = (2 * pages_per_block * page * heads * hd * 4 + 4 * n_blocks * heads * hd * 4
               + 2 * n_top * heads * LANES * 4)
    grid_spec = pltpu.PrefetchScalarGridSpec(
        num_scalar_prefetch=1,
        grid=(n_seq, n_blocks),
        in_specs=[page_spec(pp) for pp in range(pages_per_block)]
        + [pl.BlockSpec((1, heads, hd), lambda b, n, pt: (b, 0, 0))],
        out_specs=pl.BlockSpec((1, n_top, heads, LANES), lambda b, n, pt: (b, 0, 0, 0)),
        scratch_shapes=[pltpu.VMEM((n_blocks, heads, hd), F32)],
    )
    idx = pl.pallas_call(
        functools.partial(_moba_select_kernel, pages_per_block=pages_per_block, n_blocks=n_blocks,
                          n_top=n_top),
        grid_spec=grid_spec,
        out_shape=jax.ShapeDtypeStruct((n_seq, n_top, heads, LANES), jnp.int32),
        compiler_params=_params(planned, 2),
        name="moba_select",
    )(page_table, *([cache_k] * pages_per_block), q)
    return idx[..., 0]


def _moba_sample_kernel(pt_ref, idx_ref, *refs, n_sel, scale):
    k_refs = refs[:n_sel]
    v_refs = refs[n_sel:2 * n_sel]
    q_ref, kn_ref, vn_ref, o_ref = refs[2 * n_sel:]
    page, group, hd = k_refs[0].shape[1:]
    rows = page * group
    mine = lax.broadcasted_iota(jnp.int32, (SUBLANES, rows), 1) % group == pl.program_id(1) % group
    q8 = jnp.broadcast_to(q_ref[0] * scale, (SUBLANES, hd))
    q8b = q8.astype(BF16)
    scores = [jnp.where(mine, _nt(q8b, k_ref[0].reshape(rows, hd).astype(BF16)), NEG_BIG)
              for k_ref in k_refs]
    s_own = jnp.sum(q8 * kn_ref[0], axis=-1, keepdims=True)
    mx = s_own
    for s in scores:
        mx = jnp.maximum(mx, s.max(axis=-1, keepdims=True))
    p_own = jnp.exp(s_own - mx)
    denom = p_own
    acc = p_own * vn_ref[0]
    for s, v_ref in zip(scores, v_refs):
        p = jnp.exp(s - mx)
        denom = denom + jnp.sum(p, axis=-1, keepdims=True)
        acc = acc + jnp.dot(p.astype(BF16), v_ref[0].reshape(rows, hd).astype(BF16),
                            preferred_element_type=F32)
    o_ref[0] = (acc / denom)[0:1].astype(o_ref.dtype)


def _moba_sample(page_table, picks, cache_k, cache_v, q, k_new, v_new):
    n_seq, n_pages = page_table.shape
    _, page, heads, hd = cache_k.shape
    pages_per_block = MOBA_BLOCK // page
    n_top = picks.shape[1]
    n_sel = n_top * pages_per_block

    def page_spec(t, pp):
        return pl.BlockSpec(
            (1, page, SUBLANES, hd),
            lambda b, h, pt, ix: (pt[b, ix[b, t, h] * pages_per_block + pp], 0, h // SUBLANES, 0))

    page_specs = [page_spec(t, pp) for t in range(n_top) for pp in range(pages_per_block)]
    vec_spec = pl.BlockSpec((1, 1, hd), lambda b, h, pt, ix: (b, 0, h))
    planned = 2 * 2 * n_sel * page * SUBLANES * hd * 4 + 16 * page * SUBLANES * hd * 4
    grid_spec = pltpu.PrefetchScalarGridSpec(
        num_scalar_prefetch=2,
        grid=(n_seq, heads),
        in_specs=page_specs + page_specs + [vec_spec, vec_spec, vec_spec],
        out_specs=vec_spec,
    )
    return pl.pallas_call(
        functools.partial(_moba_sample_kernel, n_sel=n_sel, scale=hd ** -0.5),
        grid_spec=grid_spec,
        out_shape=jax.ShapeDtypeStruct((n_seq, 1, heads * hd), BF16),
        compiler_params=_params(planned, 2),
        name="moba_sample",
    )(page_table, picks, *([cache_k] * n_sel), *([cache_v] * n_sel), q, k_new, v_new)


def _rope_tables(pos, hd):
    half = hd // 2
    inv = ROPE_THETA ** (-jnp.arange(half, dtype=F32) / half)
    ang = pos[:, None] * inv[None, :]
    cos, sin = jnp.cos(ang), jnp.sin(ang)
    return jnp.concatenate([cos, cos], axis=-1), jnp.concatenate([-sin, sin], axis=-1)


def kernel(x_prompt, x_sample, cache_k, cache_v, page_table, state_gla, state_conv, g_mix, w_in_a,
           w_a2, b_a2, g_gla_head, w_out_a, g_kv, w_kv, w_q, w_out_b, g_ffn, w_up, w_conv, b_conv,
           w_down, g_final):
    n_b, seq, d = x_prompt.shape
    n_s, t_s, _ = x_sample.shape
    assert t_s == 1
    _, _, gla_h, dk, dv = state_gla.shape
    _, page, att_h, hd = cache_k.shape
    n_pages = page_table.shape[1]
    past_len = n_pages * page
    rank = w_a2.shape[1]
    qk_cols, v_cols = gla_h * dk, gla_h * dv
    proj_cols = 2 * qk_cols + 2 * v_cols
    da = att_h * hd
    assert rank <= LANES and len(w_in_a) == 1 and len(w_q) == 1

    xp = x_prompt.reshape(n_b * seq, d)
    xs = x_sample.reshape(n_s, d)
    rope_p = _rope_tables(jnp.arange(seq, dtype=F32), hd)
    rope_s = tuple(jnp.tile(t, (n_s, 1)) for t in _rope_tables(past_len + jnp.arange(t_s, dtype=F32), hd))

    w_a1_pad = jnp.pad(w_in_a[:, :, proj_cols:], ((0, 0), (0, 0), (0, LANES - rank)))
    w_a2_pad = jnp.pad(w_a2[0], ((0, LANES - rank), (0, 0)))
    conv_rows = state_conv.transpose(0, 2, 1, 3)
    b_conv3 = b_conv[:, None, :]
    down_split = 2 if (w_down.shape[1] // 2) % LANES == 0 else 1

    def norm(h, gains, dtype=BF16):
        return tuple(zip(_rmsnorm(h[0], gains, dtype), _rmsnorm(h[1], gains, dtype)))

    def layer0(h):
        (hn,) = norm(h, g_mix[0:1])
        proj_p, proj_s = _matmul(*hn, w_in_a, 0, proj_cols)
        a_p, a_s = _matmul(*hn, w_a1_pad, 0, LANES)
        og_p, s_p = _gla_prompt(proj_p, a_p, w_a2_pad, b_a2[0:1], g_gla_head[0:1], n_b, seq, gla_h, dk, dv)
        og_s, s_s = _gla_sample(proj_s.reshape(n_s, 1, proj_cols), a_s.reshape(n_s, 1, LANES), w_a2_pad,
                                b_a2[0:1], g_gla_head[0:1], state_gla, 0, gla_h, dk, dv)
        return _matmul(og_p, og_s.reshape(n_s, v_cols), w_out_a, 0, d, res=h), (s_p, s_s)

    def conv_ffn(h, layer):
        (hn,) = norm(h, g_ffn[layer:layer + 1])
        act_p, st_p, act_s, u_s = _up(*hn, w_up, w_conv, b_conv3, conv_rows, layer, n_b, seq)
        st_s = jnp.stack([state_conv[layer][:, 1, :], u_s], axis=1)
        return _matmul(act_p, act_s, w_down, layer, d, k_split=down_split, res=h), (st_p, st_s)

    def layer1(h):
        hn, kvn = norm(h, jnp.stack([g_mix[1], g_kv]))
        rope = (rope_p, rope_s)
        k = _matmul(*kvn, w_kv[None], 0, da, rope=rope)
        v = _matmul(*kvn, w_kv[None], 0, da, w_col0=da)
        q = _matmul(*hn, w_q, 0, da, rope=rope)
        att_p = _moba_prompt(q[0], k[0], v[0], n_b, seq, att_h, hd)
        q3, k3, v3 = (t[1].reshape(n_s, 1, da) for t in (q, k, v))
        picks = _moba_select(page_table, cache_k, q[1].reshape(n_s, att_h, hd))
        att_s = _moba_sample(page_table, picks, cache_k, cache_v, q3, k3, v3).reshape(n_s, da)
        return _matmul(att_p, att_s, w_out_b, 0, d, res=h), k, v

    h, (gla_p, gla_s) = layer0((xp, xs))
    h, conv0 = conv_ffn(h, 0)
    h, (k_p, k_s), (v_p, v_s) = layer1(h)
    h, conv1 = conv_ffn(h, 1)
    ((y_p, y_s),) = norm(h, g_final[None, :], F32)
    gla_p, gla_s = gla_p[None], gla_s[None]
    conv_p, conv_s = jnp.stack([conv0[0], conv1[0]]), jnp.stack([conv0[1], conv1[1]])
    return (y_p.reshape(n_b, seq, d), y_s.reshape(n_s, t_s, d),
            k_p.reshape(n_b, seq, att_h, hd), v_p.reshape(n_b, seq, att_h, hd), gla_p, conv_p,
            k_s.reshape(n_s, t_s, att_h, hd), v_s.reshape(n_s, t_s, att_h, hd), gla_s, conv_s)
```

```python
import functools

import jax
import jax.numpy as jnp
from jax import lax
from jax.experimental import pallas as pl
from jax.experimental.pallas import tpu as pltpu

F32 = jnp.float32
BF16 = jnp.bfloat16
HIGHEST = lax.Precision.HIGHEST

GLA_GATE_TEMP = 16.0
GLA_CHUNK = 64
MOBA_BLOCK = 256
MOBA_TOPK = 3
ROPE_THETA = 10000.0
EPS = 1e-6

LANES = 128
SUBLANES = 8
VMEM_PHYSICAL_BYTES = 64 * 1024 * 1024
VMEM_BUDGET_BYTES = 47 * 1024 * 1024
VMEM_COMPILER_SLACK_BYTES = 8 * 1024 * 1024
NEG_BIG = -1e30


def _vmem_limit(planned_bytes):
    return int(min(planned_bytes + VMEM_COMPILER_SLACK_BYTES, VMEM_PHYSICAL_BYTES - 4 * 1024 * 1024))


def _params(planned_bytes, n_axes):
    return pltpu.CompilerParams(
        dimension_semantics=("arbitrary",) * n_axes,
        vmem_limit_bytes=_vmem_limit(planned_bytes),
    )


def _nt(a, b, precision=None):
    return lax.dot_general(a, b, (((1,), (1,)), ((), ())), precision=precision,
                           preferred_element_type=F32)


def _tn(a, b, precision=None):
    return lax.dot_general(a, b, (((0,), (0,)), ((), ())), precision=precision,
                           preferred_element_type=F32)


def _sigmoid(x):
    return 1.0 / (1.0 + jnp.exp(-x))


def _log_sigmoid(z):
    return jnp.minimum(z, 0.0) - jnp.log1p(jnp.exp(-jnp.abs(z)))


def _pick(n, candidates):
    for c in candidates:
        if n % c == 0:
            return c
    return n


def _norm_kernel(x_ref, g_ref, *o_refs):
    x = x_ref[...]
    y = x * lax.rsqrt(jnp.mean(x * x, axis=-1, keepdims=True) + EPS)
    for i, o_ref in enumerate(o_refs):
        o_ref[...] = (y * g_ref[i:i + 1, :]).astype(o_ref.dtype)


def _rmsnorm(x, gains, out_dtype):
    m, d = x.shape
    n_g = gains.shape[0]
    tm = _pick(m, (256, 128, 64, 32, 16, 8))
    planned = 2 * tm * d * 4 * (1 + n_g)
    outs = pl.pallas_call(
        _norm_kernel,
        grid=(m // tm,),
        in_specs=[pl.BlockSpec((tm, d), lambda i: (i, 0)),
                  pl.BlockSpec((n_g, d), lambda i: (0, 0))],
        out_specs=[pl.BlockSpec((tm, d), lambda i: (i, 0)) for _ in range(n_g)],
        out_shape=[jax.ShapeDtypeStruct((m, d), out_dtype) for _ in range(n_g)],
        compiler_params=_params(planned, 1),
        name="rmsnorm",
    )(x, gains)
    return outs


def _rope_tile(acc, cos, sin_signed):
    heads = []
    for c in range(acc.shape[1] // LANES):
        xc = acc[:, c * LANES:(c + 1) * LANES]
        heads.append(xc * cos + pltpu.roll(xc, LANES // 2, 1) * sin_signed)
    return heads[0] if len(heads) == 1 else jnp.concatenate(heads, axis=1)


def _cast_weight_tile(w_ref, wb_ref):
    k = w_ref.shape[0]
    ck = _pick(k, (512, 256, 128))

    def body(i, carry):
        r = pl.multiple_of(i * ck, ck)
        wb_ref[pl.ds(r, ck), :] = w_ref[pl.ds(r, ck), :].astype(BF16)
        return carry

    lax.fori_loop(0, k // ck, body, 0)


def _mm_kernel(*refs, has_res, has_rope):
    x_ref, xs_ref, w_ref = refs[:3]
    pos = 3
    res_refs = rope_refs = None
    if has_res:
        res_refs = refs[pos:pos + 2]
        pos += 2
    if has_rope:
        rope_refs = refs[pos:pos + 4]
        pos += 4
    o_ref, os_ref, wb_ref = refs[pos:pos + 3]

    def project(lhs_ref, group):
        acc = jnp.dot(lhs_ref[...], wb_ref[...], preferred_element_type=F32)
        if has_rope:
            acc = _rope_tile(acc, rope_refs[2 * group][...], rope_refs[2 * group + 1][...])
        if has_res:
            acc = acc + res_refs[group][...]
        return acc

    @pl.when(pl.program_id(1) == 0)
    def _():
        _cast_weight_tile(w_ref, wb_ref)
        os_ref[...] = project(xs_ref, 1).astype(os_ref.dtype)

    o_ref[...] = project(x_ref, 0).astype(o_ref.dtype)


def _mm_tiles(m, k, n, has_res, row_period):
    for tm, tn in ((1024, 512), (512, 512), (512, 256), (256, 256), (256, 128), (128, 128),
                   (64, 128), (32, 128), (16, 128), (8, 128)):
        tm = min(tm, m)
        if m % tm or n % tn or row_period % tm:
            continue
        planned = (2 * k * tn * 4 + k * tn * 2 + 2 * tm * k * 2
                   + (2 + 2 * has_res + 1) * tm * tn * 4)
        if planned <= VMEM_BUDGET_BYTES:
            return tm, tn, planned
    raise ValueError(f"no matmul tiling for {(m, k, n)}")


def _matmul(x, xs, w, layer, n_out, *, w_col0=0, k_split=1, res=None, rope=None, out_dtype=F32):
    if k_split > 1:
        assert rope is None
        for part in range(k_split):
            res = _matmul_part(x, xs, w, layer, n_out, w_col0, k_split, part, res, None,
                               out_dtype if part == k_split - 1 else F32)
        return res
    return _matmul_part(x, xs, w, layer, n_out, w_col0, 1, 0, res, rope, out_dtype)


def _matmul_part(x, xs, w, layer, n_out, w_col0, k_split, part, res, rope, out_dtype):
    m, ms = x.shape[0], xs.shape[0]
    k = x.shape[1] // k_split
    assert x.shape[1] % k_split == 0 and k % LANES == 0
    tm, tn, planned = _mm_tiles(m, k, n_out, res is not None, m if rope is None else rope[0][0].shape[0])
    assert w_col0 % tn == 0
    col0 = w_col0 // tn
    row_tile = pl.BlockSpec((tm, tn), lambda j, i: (i, j))
    sample_tile = pl.BlockSpec((ms, tn), lambda j, i: (0, j))
    in_specs = [pl.BlockSpec((tm, k), lambda j, i: (i, part)),
                pl.BlockSpec((ms, k), lambda j, i: (0, part)),
                pl.BlockSpec((None, k, tn), lambda j, i: (layer, part, j + col0))]
    args = [x, xs, w]
    if res is not None:
        in_specs += [row_tile, sample_tile]
        args += list(res)
    if rope is not None:
        (cos, sin_signed), (cos_s, sin_s) = rope
        p_tiles = cos.shape[0] // tm
        assert cos.shape[0] % tm == 0 and tn % LANES == 0 and cos_s.shape[0] == ms
        in_specs += [pl.BlockSpec((tm, LANES), lambda j, i: (i % p_tiles, 0))] * 2
        in_specs += [pl.BlockSpec((ms, LANES), lambda j, i: (0, 0))] * 2
        args += [cos, sin_signed, cos_s, sin_s]
    return pl.pallas_call(
        functools.partial(_mm_kernel, has_res=res is not None, has_rope=rope is not None),
        grid=(n_out // tn, m // tm),
        in_specs=in_specs,
        out_specs=[row_tile, sample_tile],
        out_shape=[jax.ShapeDtypeStruct((m, n_out), out_dtype),
                   jax.ShapeDtypeStruct((ms, n_out), out_dtype)],
        scratch_shapes=[pltpu.VMEM((k, tn), BF16)],
        compiler_params=_params(planned, 2),
        name="matmul",
    )(*args)


def _gla_gate_norm(o, r, gh):
    y = o * lax.rsqrt(jnp.mean(o * o, axis=-1, keepdims=True) + EPS) * gh
    return y * (r * _sigmoid(r))


def _gla_gate_kernel(a_ref, wa_ref, ba_ref, g_ref):
    z = jnp.dot(a_ref[...], wa_ref[...], precision=HIGHEST, preferred_element_type=F32) + ba_ref[...]
    g_ref[...] = _log_sigmoid(z) / GLA_GATE_TEMP


def _gla_gate(a_pad, wa_pad, b_a2):
    m = a_pad.shape[0]
    n = wa_pad.shape[1]
    tm = _pick(m, (512, 256, 128, 64, 32, 16, 8))
    planned = 2 * tm * LANES * 4 + 2 * LANES * n * 4 + 4 * tm * n * 4
    return pl.pallas_call(
        _gla_gate_kernel,
        grid=(m // tm,),
        in_specs=[pl.BlockSpec((tm, LANES), lambda i: (i, 0)),
                  pl.BlockSpec((LANES, n), lambda i: (0, 0)),
                  pl.BlockSpec((1, n), lambda i: (0, 0))],
        out_specs=pl.BlockSpec((tm, n), lambda i: (i, 0)),
        out_shape=jax.ShapeDtypeStruct((m, n), F32),
        compiler_params=_params(planned, 1),
        name="gla_gate",
    )(a_pad, wa_pad, b_a2)


def _gla_prompt_kernel(q_ref, k_ref, v_ref, r_ref, g_ref, gh_ref, o_ref, s_ref, st_ref, *, dk):
    c = pl.program_id(2)

    @pl.when(c == 0)
    def _():
        st_ref[...] = jnp.zeros_like(st_ref)

    n_c = q_ref.shape[0]
    g = g_ref[...]
    row = lax.broadcasted_iota(jnp.int32, (n_c, n_c), 0)
    col = lax.broadcasted_iota(jnp.int32, (n_c, n_c), 1)
    causal = col <= row
    b = jnp.dot(causal.astype(F32), g, precision=HIGHEST, preferred_element_type=F32)
    b_last = b[n_c - 1:n_c, :]

    q = q_ref[...] * (dk ** -0.5)
    k = k_ref[...]
    v = v_ref[...].astype(BF16)
    qe = (q * jnp.exp(b)).astype(BF16)
    kd = (k * jnp.exp(-b)).astype(BF16)
    kr = (k * jnp.exp(b_last - b)).astype(BF16)
    att = jnp.where(causal, _nt(qe, kd), 0.0).astype(BF16)
    st_old = st_ref[...]
    o = jnp.dot(att, v, preferred_element_type=F32) + _nt(qe, st_old.astype(BF16))
    st_new = st_old * jnp.exp(b_last) + _tn(v, kr)
    st_ref[...] = st_new
    o_ref[...] = _gla_gate_norm(o, r_ref[...], gh_ref[...]).astype(o_ref.dtype)

    @pl.when(c == pl.num_programs(2) - 1)
    def _():
        s_ref[0, 0] = st_new.T


def _gla_prompt(proj, gate, g_head, batch, seq, heads, dk, dv):
    assert seq % GLA_CHUNK == 0 and dv == 2 * dk
    n_chunks = seq // GLA_CHUNK
    m = batch * seq
    rows = lambda b, h, c: b * n_chunks + c
    planned = 2 * (3 * GLA_CHUNK * dk + 3 * GLA_CHUNK * dv + 2 * dk * dv) * 4 + 7 * dk * dv * 4
    return pl.pallas_call(
        functools.partial(_gla_prompt_kernel, dk=dk),
        grid=(batch, heads, n_chunks),
        in_specs=[
            pl.BlockSpec((GLA_CHUNK, dk), lambda b, h, c: (rows(b, h, c), h)),
            pl.BlockSpec((GLA_CHUNK, dk), lambda b, h, c: (rows(b, h, c), heads + h)),
            pl.BlockSpec((GLA_CHUNK, dv), lambda b, h, c: (rows(b, h, c), heads + h)),
            pl.BlockSpec((GLA_CHUNK, dv), lambda b, h, c: (rows(b, h, c), 2 * heads + h)),
            pl.BlockSpec((GLA_CHUNK, dk), lambda b, h, c: (rows(b, h, c), h)),
            pl.BlockSpec((1, dv), lambda b, h, c: (0, 0)),
        ],
        out_specs=[
            pl.BlockSpec((GLA_CHUNK, dv), lambda b, h, c: (rows(b, h, c), h)),
            pl.BlockSpec((1, 1, dk, dv), lambda b, h, c: (b, h, 0, 0)),
        ],
        out_shape=[jax.ShapeDtypeStruct((m, heads * dv), BF16),
                   jax.ShapeDtypeStruct((batch, heads, dk, dv), F32)],
        scratch_shapes=[pltpu.VMEM((dv, dk), F32)],
        compiler_params=_params(planned, 3),
        name="gla_prompt",
    )(proj, proj, proj, proj, gate, g_head)


def _gla_sample_kernel(q_ref, k_ref, v_ref, r_ref, g_ref, gh_ref, s0_ref, o_ref, s_ref, *, dk):
    g = g_ref[0]
    q = q_ref[0] * (dk ** -0.5)
    k = k_ref[0]
    stacked = jnp.concatenate([q, k, g, jnp.zeros((SUBLANES - 3, q.shape[1]), F32)], axis=0)
    cols = stacked.T
    qc, kc, gc = cols[:, 0:1], cols[:, 1:2], cols[:, 2:3]
    s_new = jnp.exp(gc) * s0_ref[0, 0, 0] + kc * v_ref[0]
    s_ref[0, 0] = s_new
    o = jnp.sum(qc * s_new, axis=0, keepdims=True)
    o_ref[0] = _gla_gate_norm(o, r_ref[0], gh_ref[...]).astype(o_ref.dtype)


def _gla_sample(proj, gate, g_head, state, layer, heads, dk, dv):
    n_seq = proj.shape[0]
    planned = 2 * 2 * dk * dv * 4 + 4 * dk * dv * 4
    return pl.pallas_call(
        functools.partial(_gla_sample_kernel, dk=dk),
        grid=(n_seq, heads),
        in_specs=[
            pl.BlockSpec((1, 1, dk), lambda b, h: (b, 0, h)),
            pl.BlockSpec((1, 1, dk), lambda b, h: (b, 0, heads + h)),
            pl.BlockSpec((1, 1, dv), lambda b, h: (b, 0, heads + h)),
            pl.BlockSpec((1, 1, dv), lambda b, h: (b, 0, 2 * heads + h)),
            pl.BlockSpec((1, 1, dk), lambda b, h: (b, 0, h)),
            pl.BlockSpec((1, dv), lambda b, h: (0, 0)),
            pl.BlockSpec((1, 1, 1, dk, dv), lambda b, h: (layer, b, h, 0, 0)),
        ],
        out_specs=[
            pl.BlockSpec((1, 1, dv), lambda b, h: (b, 0, h)),
            pl.BlockSpec((1, 1, dk, dv), lambda b, h: (b, h, 0, 0)),
        ],
        out_shape=[jax.ShapeDtypeStruct((n_seq, 1, heads * dv), BF16),
                   jax.ShapeDtypeStruct((n_seq, heads, dk, dv), F32)],
        compiler_params=_params(planned, 2),
        name="gla_sample",
    )(proj, proj, proj, proj, gate, g_head, state)


def _conv_gate(ua, ug, wca, wcg, bca, bcg):
    ca = bca + wca[0:1] * ua[0] + wca[1:2] * ua[1] + wca[2:3] * ua[2]
    cg = bcg + wcg[0:1] * ug[0] + wcg[1:2] * ug[1] + wcg[2:3] * ug[2]
    return ca * _sigmoid(ca) * cg


def _up_kernel(x_ref, xs_ref, wa_ref, wg_ref, wca_ref, wcg_ref, bca_ref, bcg_ref, sa_ref, sg_ref,
               act_ref, st_ref, acts_ref, us_ref, wba_ref, wbg_ref, ua_ref, ug_ref, *, tiles_per_seq):
    i = pl.program_id(1)
    tm = x_ref.shape[0]
    conv = lambda ua, ug: _conv_gate(ua, ug, wca_ref[...], wcg_ref[...], bca_ref[...], bcg_ref[...])

    @pl.when(i == 0)
    def _():
        _cast_weight_tile(wa_ref, wba_ref)
        _cast_weight_tile(wg_ref, wbg_ref)
        xs = xs_ref[...]
        ua = jnp.dot(xs, wba_ref[...], preferred_element_type=F32)
        ug = jnp.dot(xs, wbg_ref[...], preferred_element_type=F32)
        acts_ref[...] = conv((sa_ref[0], sa_ref[1], ua), (sg_ref[0], sg_ref[1], ug)).astype(acts_ref.dtype)
        us_ref[0] = ua
        us_ref[1] = ug

    @pl.when(i % tiles_per_seq == 0)
    def _():
        ua_ref[0:SUBLANES, :] = jnp.zeros((SUBLANES, ua_ref.shape[1]), F32)
        ug_ref[0:SUBLANES, :] = jnp.zeros((SUBLANES, ug_ref.shape[1]), F32)

    x = x_ref[...]
    ua_ref[SUBLANES:, :] = jnp.dot(x, wba_ref[...], preferred_element_type=F32)
    ug_ref[SUBLANES:, :] = jnp.dot(x, wbg_ref[...], preferred_element_type=F32)
    taps = lambda u_ref: tuple(u_ref[pl.ds(SUBLANES - 2 + d, tm), :] for d in range(3))
    act_ref[...] = conv(taps(ua_ref), taps(ug_ref)).astype(act_ref.dtype)
    st_ref[0, 0] = ua_ref[pl.ds(tm + SUBLANES - 2, 2), :]
    st_ref[0, 1] = ug_ref[pl.ds(tm + SUBLANES - 2, 2), :]
    ua_ref[0:SUBLANES, :] = ua_ref[pl.ds(tm, SUBLANES), :]
    ug_ref[0:SUBLANES, :] = ug_ref[pl.ds(tm, SUBLANES), :]


def _up_tiles(m, k, f, seq):
    for tm, tn in ((1024, 256), (512, 256), (512, 128), (256, 128), (128, 128)):
        if seq % tm or f % tn:
            continue
        planned = (2 * 2 * k * tn * 4 + 2 * k * tn * 2 + 2 * tm * k * 2
                   + 2 * (tm + SUBLANES) * tn * 4 + 2 * tm * tn * 2 + 4 * tm * tn * 4)
        if planned <= VMEM_BUDGET_BYTES:
            return tm, tn, planned
    raise ValueError(f"no up-projection tiling for {(m, k, f)}")


def _up(xn, xs, w_up, w_conv, b_conv, conv_rows, layer, batch, seq):
    m, k = xn.shape
    ms = xs.shape[0]
    f = w_up.shape[2] // 2
    assert w_conv.shape[1] == 3
    tm, tn, planned = _up_tiles(m, k, f, seq)
    g0 = f // tn
    tiles_per_seq = seq // tm
    half = lambda block, off: pl.BlockSpec((None,) + block, lambda j, i: (layer,) + (0,) * (len(block) - 1) + (j + off,))
    act, st, act_s, u_s = pl.pallas_call(
        functools.partial(_up_kernel, tiles_per_seq=tiles_per_seq),
        grid=(f // tn, m // tm),
        in_specs=[
            pl.BlockSpec((tm, k), lambda j, i: (i, 0)),
            pl.BlockSpec((ms, k), lambda j, i: (0, 0)),
            half((k, tn), 0), half((k, tn), g0),
            half((3, tn), 0), half((3, tn), g0),
            half((1, tn), 0), half((1, tn), g0),
            half((2, ms, tn), 0), half((2, ms, tn), g0),
        ],
        out_specs=[
            pl.BlockSpec((tm, tn), lambda j, i: (i, j)),
            pl.BlockSpec((1, 2, 2, tn), lambda j, i: (i // tiles_per_seq, 0, 0, j)),
            pl.BlockSpec((ms, tn), lambda j, i: (0, j)),
            pl.BlockSpec((2, ms, tn), lambda j, i: (0, 0, j)),
        ],
        out_shape=[jax.ShapeDtypeStruct((m, f), BF16),
                   jax.ShapeDtypeStruct((batch, 2, 2, f), F32),
                   jax.ShapeDtypeStruct((ms, f), BF16),
                   jax.ShapeDtypeStruct((2, ms, f), F32)],
        scratch_shapes=[pltpu.VMEM((k, tn), BF16), pltpu.VMEM((k, tn), BF16),
                        pltpu.VMEM((tm + SUBLANES, tn), F32), pltpu.VMEM((tm + SUBLANES, tn), F32)],
        compiler_params=_params(planned, 2),
        name="up_conv",
    )(xn, xs, w_up, w_up, w_conv, w_conv, b_conv, b_conv, conv_rows, conv_rows)
    return (act, st.transpose(0, 2, 1, 3).reshape(batch, 2, 2 * f),
            act_s, u_s.transpose(1, 0, 2).reshape(ms, 2 * f))


def _moba_prompt_kernel(q_ref, k_ref, v_ref, o_ref, kb_ref, vt_ref, mean_ref, p_ref,
                        *, n_blocks, scale):
    blk = MOBA_BLOCK
    hd = q_ref.shape[1]
    mean_ref[...] = jnp.zeros_like(mean_ref)
    for j in range(n_blocks):
        rows = slice(j * blk, (j + 1) * blk)
        kf = k_ref[rows, :]
        kb_ref[rows, :] = kf.astype(BF16)
        mean_ref[j:j + 1, :] = jnp.mean(kf, axis=0, keepdims=True)
        vt_ref[0:hd, rows] = v_ref[rows, :].T.astype(BF16)
    vt_ref[hd:2 * hd, :] = jnp.ones((hd, vt_ref.shape[1]), BF16)

    key_l = lax.broadcasted_iota(jnp.int32, (blk, blk), 0)
    qry_l = lax.broadcasted_iota(jnp.int32, (blk, blk), 1)
    causal = key_l <= qry_l
    for i in range(n_blocks):
        n_keys = (i + 1) * blk
        qf = q_ref[i * blk:(i + 1) * blk, :]
        s_t = _nt(kb_ref[0:n_keys, :], (qf * scale).astype(BF16))
        parts = [s_t[j * blk:(j + 1) * blk, :] for j in range(i + 1)]
        parts[i] = jnp.where(causal, parts[i], NEG_BIG)
        if i > MOBA_TOPK:
            gate_t = _nt(mean_ref[...], qf, precision=HIGHEST)
            blk_id = lax.broadcasted_iota(jnp.int32, gate_t.shape, 0)
            for j in range(i):
                gj = gate_t[j:j + 1, :]
                beats = (blk_id < i) & ((gate_t > gj) | ((gate_t == gj) & (blk_id < j)))
                ahead = jnp.sum(jnp.where(beats, 1.0, 0.0), axis=0, keepdims=True)
                parts[j] = jnp.where(ahead < MOBA_TOPK, parts[j], NEG_BIG)
        mx = parts[0].max(axis=0, keepdims=True)
        for part in parts[1:]:
            mx = jnp.maximum(mx, part.max(axis=0, keepdims=True))
        for j, part in enumerate(parts):
            p_ref[j * blk:(j + 1) * blk, :] = jnp.exp(part - mx).astype(BF16)
        o_t = jnp.dot(vt_ref[:, 0:n_keys], p_ref[0:n_keys, :], preferred_element_type=F32)
        o_t = o_t[0:hd, :] / o_t[hd:hd + 1, :]
        o_ref[i * blk:(i + 1) * blk, :] = o_t.T.astype(o_ref.dtype)


def _moba_prompt(q, k, v, batch, seq, heads, hd):
    assert seq % MOBA_BLOCK == 0 and hd == LANES
    n_blocks = seq // MOBA_BLOCK
    mean_rows = -(-n_blocks // SUBLANES) * SUBLANES
    planned = (2 * 3 * seq * hd * 4 + 2 * seq * hd * 2 + seq * hd * 2 + 2 * hd * seq * 2
               + seq * MOBA_BLOCK * 2 + 4 * seq * MOBA_BLOCK * 4)
    head_rows = pl.BlockSpec((seq, hd), lambda b, h: (b, h))
    return pl.pallas_call(
        functools.partial(_moba_prompt_kernel, n_blocks=n_blocks, scale=hd ** -0.5),
        grid=(batch, heads),
        in_specs=[head_rows, head_rows, head_rows],
        out_specs=head_rows,
        out_shape=jax.ShapeDtypeStruct((batch * seq, heads * hd), BF16),
        scratch_shapes=[pltpu.VMEM((seq, hd), BF16), pltpu.VMEM((2 * hd, seq), BF16),
                        pltpu.VMEM((mean_rows, hd), F32), pltpu.VMEM((seq, MOBA_BLOCK), BF16)],
        compiler_params=_params(planned, 2),
        name="moba_prompt",
    )(q, k, v)


def _moba_select_kernel(pt_ref, *refs, pages_per_block, blocks_per_step, n_blocks, n_top):
    n_pages_in = pages_per_block * blocks_per_step
    ck_refs = refs[:n_pages_in]
    q_ref, idx_ref, mean_ref = refs[n_pages_in:]
    step = pl.program_id(1)
    page, heads, hd = ck_refs[0].shape[1:]
    for t in range(blocks_per_step):
        block_refs = ck_refs[t * pages_per_block:(t + 1) * pages_per_block]
        total = jnp.sum(block_refs[0][0], axis=0)
        for ck_ref in block_refs[1:]:
            total = total + jnp.sum(ck_ref[0], axis=0)
        mean_ref[step * blocks_per_step + t] = total * (1.0 / (pages_per_block * page))

    @pl.when(step == pl.num_programs(1) - 1)
    def _():
        prod = (mean_ref[...] * q_ref[...]).reshape(n_blocks * heads, hd)
        gate = jnp.dot(prod, jnp.ones((hd, LANES), F32), precision=HIGHEST,
                       preferred_element_type=F32).reshape(n_blocks, heads, LANES)
        blk_id = lax.broadcasted_iota(jnp.int32, gate.shape, 0)
        for t in range(n_top):
            best = gate.max(axis=0)
            first = jnp.where(gate == best[None], blk_id, n_blocks).min(axis=0)
            idx_ref[0, t] = first
            gate = jnp.where(blk_id == first[None], -jnp.inf, gate)


def _moba_select(page_table, cache_k, q):
    n_seq, n_pages = page_table.shape
    _, page, heads, hd = cache_k.shape
    pages_per_block = MOBA_BLOCK // page
    assert n_pages % pages_per_block == 0 and heads % SUBLANES == 0
    n_blocks = n_pages // pages_per_block
    n_top = min(MOBA_TOPK, n_blocks)
    blocks_per_step = 2 if n_blocks % 2 == 0 else 1
    pages_per_step = pages_per_block * blocks_per_step

    def page_spec(pp):
        return pl.BlockSpec((1, page, heads, hd),
                            lambda b, n, pt: (pt[b, n * pages_per_step + pp], 0, 0, 0))

    planned = (2 * pages_per_step * page * heads * hd * 4 + 4 * n_blocks * heads * hd * 4
               + 2 * n_top * heads * LANES * 4)
    grid_spec = pltpu.PrefetchScalarGridSpec(
        num_scalar_prefetch=1,
        grid=(n_seq, n_blocks // blocks_per_step),
        in_specs=[page_spec(pp) for pp in range(pages_per_step)]
        + [pl.BlockSpec((1, heads, hd), lambda b, n, pt: (b, 0, 0))],
        out_specs=pl.BlockSpec((1, n_top, heads, LANES), lambda b, n, pt: (b, 0, 0, 0)),
        scratch_shapes=[pltpu.VMEM((n_blocks, heads, hd), F32)],
    )
    idx = pl.pallas_call(
        functools.partial(_moba_select_kernel, pages_per_block=pages_per_block,
                          blocks_per_step=blocks_per_step, n_blocks=n_blocks, n_top=n_top),
        grid_spec=grid_spec,
        out_shape=jax.ShapeDtypeStruct((n_seq, n_top, heads, LANES), jnp.int32),
        compiler_params=_params(planned, 2),
        name="moba_select",
    )(page_table, *([cache_k] * pages_per_step), q)
    return idx[..., 0]


def _moba_sample_kernel(pt_ref, idx_ref, *refs, n_sel, scale):
    k_refs = refs[:n_sel]
    v_refs = refs[n_sel:2 * n_sel]
    q_ref, kn_ref, vn_ref, o_ref = refs[2 * n_sel:]
    page, group, hd = k_refs[0].shape[1:]
    rows = page * group
    mine = lax.broadcasted_iota(jnp.int32, (SUBLANES, rows), 1) % group == pl.program_id(1) % group
    q8 = jnp.broadcast_to(q_ref[0] * scale, (SUBLANES, hd))
    q8b = q8.astype(BF16)
    scores = [jnp.where(mine, _nt(q8b, k_ref[0].reshape(rows, hd).astype(BF16)), NEG_BIG)
              for k_ref in k_refs]
    s_own = jnp.sum(q8 * kn_ref[0], axis=-1, keepdims=True)
    mx = s_own
    for s in scores:
        mx = jnp.maximum(mx, s.max(axis=-1, keepdims=True))
    p_own = jnp.exp(s_own - mx)
    denom = p_own
    acc = p_own * vn_ref[0]
    for s, v_ref in zip(scores, v_refs):
        p = jnp.exp(s - mx)
        denom = denom + jnp.sum(p, axis=-1, keepdims=True)
        acc = acc + jnp.dot(p.astype(BF16), v_ref[0].reshape(rows, hd).astype(BF16),
                            preferred_element_type=F32)
    o_ref[0] = (acc / denom)[0:1].astype(o_ref.dtype)


def _moba_sample(page_table, picks, cache_k, cache_v, q, k_new, v_new):
    n_seq, n_pages = page_table.shape
    _, page, heads, hd = cache_k.shape
    pages_per_block = MOBA_BLOCK // page
    n_top = picks.shape[1]
    n_sel = n_top * pages_per_block

    def page_spec(t, pp):
        return pl.BlockSpec(
            (1, page, SUBLANES, hd),
            lambda b, h, pt, ix: (pt[b, ix[b, t, h] * pages_per_block + pp], 0, h // SUBLANES, 0))

    page_specs = [page_spec(t, pp) for t in range(n_top) for pp in range(pages_per_block)]
    vec_spec = pl.BlockSpec((1, 1, hd), lambda b, h, pt, ix: (b, 0, h))
    planned = 2 * 2 * n_sel * page * SUBLANES * hd * 4 + 16 * page * SUBLANES * hd * 4
    grid_spec = pltpu.PrefetchScalarGridSpec(
        num_scalar_prefetch=2,
        grid=(n_seq, heads),
        in_specs=page_specs + page_specs + [vec_spec, vec_spec, vec_spec],
        out_specs=vec_spec,
    )
    return pl.pallas_call(
        functools.partial(_moba_sample_kernel, n_sel=n_sel, scale=hd ** -0.5),
        grid_spec=grid_spec,
        out_shape=jax.ShapeDtypeStruct((n_seq, 1, heads * hd), BF16),
        compiler_params=_params(planned, 2),
        name="moba_sample",
    )(page_table, picks, *([cache_k] * n_sel), *([cache_v] * n_sel), q, k_new, v_new)


def _rope_tables(pos, hd):
    half = hd // 2
    inv = ROPE_THETA ** (-jnp.arange(half, dtype=F32) / half)
    ang = pos[:, None] * inv[None, :]
    cos, sin = jnp.cos(ang), jnp.sin(ang)
    return jnp.concatenate([cos, cos], axis=-1), jnp.concatenate([-sin, sin], axis=-1)


def kernel(x_prompt, x_sample, cache_k, cache_v, page_table, state_gla, state_conv, g_mix, w_in_a,
           w_a2, b_a2, g_gla_head, w_out_a, g_kv, w_kv, w_q, w_out_b, g_ffn, w_up, w_conv, b_conv,
           w_down, g_final):
    n_b, seq, d = x_prompt.shape
    n_s, t_s, _ = x_sample.shape
    assert t_s == 1
    _, _, gla_h, dk, dv = state_gla.shape
    _, page, att_h, hd = cache_k.shape
    n_pages = page_table.shape[1]
    past_len = n_pages * page
    rank = w_a2.shape[1]
    qk_cols, v_cols = gla_h * dk, gla_h * dv
    proj_cols = 2 * qk_cols + 2 * v_cols
    da = att_h * hd
    assert rank <= LANES and len(w_in_a) == 1 and len(w_q) == 1

    xp = x_prompt.reshape(n_b * seq, d)
    xs = x_sample.reshape(n_s, d)
    rope_p = _rope_tables(jnp.arange(seq, dtype=F32), hd)
    rope_s = tuple(jnp.tile(t, (n_s, 1)) for t in _rope_tables(past_len + jnp.arange(t_s, dtype=F32), hd))

    w_a1_pad = jnp.pad(w_in_a[:, :, proj_cols:], ((0, 0), (0, 0), (0, LANES - rank)))
    w_a2_pad = jnp.pad(w_a2[0], ((0, LANES - rank), (0, 0)))
    conv_rows = state_conv.transpose(0, 2, 1, 3)
    b_conv3 = b_conv[:, None, :]
    down_split = 2 if (w_down.shape[1] // 2) % LANES == 0 else 1

    def norm(h, gains, dtype=BF16):
        return tuple(zip(_rmsnorm(h[0], gains, dtype), _rmsnorm(h[1], gains, dtype)))

    def layer0(h):
        (hn,) = norm(h, g_mix[0:1])
        proj_p, proj_s = _matmul(*hn, w_in_a, 0, proj_cols)
        a_p, a_s = _matmul(*hn, w_a1_pad, 0, LANES)
        gate_p = _gla_gate(a_p, w_a2_pad, b_a2[0:1])
        gate_s = _gla_gate(a_s, w_a2_pad, b_a2[0:1])
        og_p, s_p = _gla_prompt(proj_p, gate_p, g_gla_head[0:1], n_b, seq, gla_h, dk, dv)
        og_s, s_s = _gla_sample(proj_s.reshape(n_s, 1, proj_cols), gate_s.reshape(n_s, 1, qk_cols),
                                g_gla_head[0:1], state_gla, 0, gla_h, dk, dv)
        return _matmul(og_p, og_s.reshape(n_s, v_cols), w_out_a, 0, d, res=h), (s_p, s_s)

    def conv_ffn(h, layer):
        (hn,) = norm(h, g_ffn[layer:layer + 1])
        act_p, st_p, act_s, u_s = _up(*hn, w_up, w_conv, b_conv3, conv_rows, layer, n_b, seq)
        st_s = jnp.stack([state_conv[layer][:, 1, :], u_s], axis=1)
        return _matmul(act_p, act_s, w_down, layer, d, k_split=down_split, res=h), (st_p, st_s)

    def layer1(h):
        hn, kvn = norm(h, jnp.stack([g_mix[1], g_kv]))
        rope = (rope_p, rope_s)
        k = _matmul(*kvn, w_kv[None], 0, da, rope=rope)
        v = _matmul(*kvn, w_kv[None], 0, da, w_col0=da)
        q = _matmul(*hn, w_q, 0, da, rope=rope)
        att_p = _moba_prompt(q[0], k[0], v[0], n_b, seq, att_h, hd)
        q3, k3, v3 = (t[1].reshape(n_s, 1, da) for t in (q, k, v))
        picks = _moba_select(page_table, cache_k, q[1].reshape(n_s, att_h, hd))
        att_s = _moba_sample(page_table, picks, cache_k, cache_v, q3, k3, v3).reshape(n_s, da)
        return _matmul(att_p, att_s, w_out_b, 0, d, res=h), k, v

    h, (gla_p, gla_s) = layer0((xp, xs))
    h, conv0 = conv_ffn(h, 0)
    h, (k_p, k_s), (v_p, v_s) = layer1(h)
    h, conv1 = conv_ffn(h, 1)
    ((y_p, y_s),) = norm(h, g_final[None, :], F32)
    gla_p, gla_s = gla_p[None], gla_s[None]
    conv_p, conv_s = jnp.stack([conv0[0], conv1[0]]), jnp.stack([conv0[1], conv1[1]])
    return (y_p.reshape(n_b, seq, d), y_s.reshape(n_s, t_s, d),
            k_p.reshape(n_b, seq, att_h, hd), v_p.reshape(n_b, seq, att_h, hd), gla_p, conv_p,
            k_s.reshape(n_s, t_s, att_h, hd), v_s.reshape(n_s, t_s, att_h, hd), gla_s, conv_s)
```

```python
import functools

import jax
import jax.numpy as jnp
from jax import lax
from jax.experimental import pallas as pl
from jax.experimental.pallas import tpu as pltpu

F32 = jnp.float32
BF16 = jnp.bfloat16
HIGHEST = lax.Precision.HIGHEST

GLA_GATE_TEMP = 16.0
GLA_CHUNK = 64
MOBA_BLOCK = 256
MOBA_TOPK = 3
ROPE_THETA = 10000.0
EPS = 1e-6

LANES = 128
SUBLANES = 8
VMEM_PHYSICAL_BYTES = 64 * 1024 * 1024
VMEM_BUDGET_BYTES = 47 * 1024 * 1024
VMEM_COMPILER_SLACK_BYTES = 8 * 1024 * 1024
NEG_BIG = -1e30


def _vmem_limit(planned_bytes):
    return int(min(planned_bytes + VMEM_COMPILER_SLACK_BYTES, VMEM_PHYSICAL_BYTES - 4 * 1024 * 1024))


def _params(planned_bytes, n_axes):
    return pltpu.CompilerParams(
        dimension_semantics=("arbitrary",) * n_axes,
        vmem_limit_bytes=_vmem_limit(planned_bytes),
    )


def _nt(a, b, precision=None):
    return lax.dot_general(a, b, (((1,), (1,)), ((), ())), precision=precision,
                           preferred_element_type=F32)


def _tn(a, b, precision=None):
    return lax.dot_general(a, b, (((0,), (0,)), ((), ())), precision=precision,
                           preferred_element_type=F32)


def _sigmoid(x):
    return 1.0 / (1.0 + jnp.exp(-x))


def _log_sigmoid(z):
    return jnp.minimum(z, 0.0) - jnp.log1p(jnp.exp(-jnp.abs(z)))


def _pick(n, candidates):
    for c in candidates:
        if n % c == 0:
            return c
    return n


def _norm_kernel(x_ref, g_ref, *o_refs):
    x = x_ref[...]
    y = x * lax.rsqrt(jnp.mean(x * x, axis=-1, keepdims=True) + EPS)
    for i, o_ref in enumerate(o_refs):
        o_ref[...] = (y * g_ref[i:i + 1, :]).astype(o_ref.dtype)


def _rmsnorm(x, gains, out_dtype):
    m, d = x.shape
    n_g = gains.shape[0]
    tm = _pick(m, (256, 128, 64, 32, 16, 8))
    planned = 2 * tm * d * 4 * (1 + n_g)
    outs = pl.pallas_call(
        _norm_kernel,
        grid=(m // tm,),
        in_specs=[pl.BlockSpec((tm, d), lambda i: (i, 0)),
                  pl.BlockSpec((n_g, d), lambda i: (0, 0))],
        out_specs=[pl.BlockSpec((tm, d), lambda i: (i, 0)) for _ in range(n_g)],
        out_shape=[jax.ShapeDtypeStruct((m, d), out_dtype) for _ in range(n_g)],
        compiler_params=_params(planned, 1),
        name="rmsnorm",
    )(x, gains)
    return outs


def _rope_tile(acc, cos, sin_signed):
    heads = []
    for c in range(acc.shape[1] // LANES):
        xc = acc[:, c * LANES:(c + 1) * LANES]
        heads.append(xc * cos + pltpu.roll(xc, LANES // 2, 1) * sin_signed)
    return heads[0] if len(heads) == 1 else jnp.concatenate(heads, axis=1)


def _cast_weight_tile(w_ref, wb_ref):
    k = w_ref.shape[0]
    ck = _pick(k, (512, 256, 128))

    def body(i, carry):
        r = pl.multiple_of(i * ck, ck)
        wb_ref[pl.ds(r, ck), :] = w_ref[pl.ds(r, ck), :].astype(BF16)
        return carry

    lax.fori_loop(0, k // ck, body, 0)


def _mm_kernel(*refs, has_res, has_rope):
    x_ref, xs_ref, w_ref = refs[:3]
    pos = 3
    res_refs = rope_refs = None
    if has_res:
        res_refs = refs[pos:pos + 2]
        pos += 2
    if has_rope:
        rope_refs = refs[pos:pos + 4]
        pos += 4
    o_ref, os_ref, wb_ref = refs[pos:pos + 3]

    def project(lhs_ref, group):
        acc = jnp.dot(lhs_ref[...], wb_ref[...], preferred_element_type=F32)
        if has_rope:
            acc = _rope_tile(acc, rope_refs[2 * group][...], rope_refs[2 * group + 1][...])
        if has_res:
            acc = acc + res_refs[group][...]
        return acc

    @pl.when(pl.program_id(1) == 0)
    def _():
        _cast_weight_tile(w_ref, wb_ref)
        os_ref[...] = project(xs_ref, 1).astype(os_ref.dtype)

    o_ref[...] = project(x_ref, 0).astype(o_ref.dtype)


def _mm_tiles(m, k, n, has_res, row_period):
    for tm, tn in ((1024, 512), (512, 512), (512, 256), (256, 256), (256, 128), (128, 128),
                   (64, 128), (32, 128), (16, 128), (8, 128)):
        tm = min(tm, m)
        if m % tm or n % tn or row_period % tm:
            continue
        planned = (2 * k * tn * 4 + k * tn * 2 + 2 * tm * k * 2
                   + (2 + 2 * has_res + 1) * tm * tn * 4)
        if planned <= VMEM_BUDGET_BYTES:
            return tm, tn, planned
    raise ValueError(f"no matmul tiling for {(m, k, n)}")


def _matmul(x, xs, w, layer, n_out, *, w_col0=0, k_split=1, res=None, rope=None, out_dtype=F32):
    if k_split > 1:
        assert rope is None
        for part in range(k_split):
            res = _matmul_part(x, xs, w, layer, n_out, w_col0, k_split, part, res, None,
                               out_dtype if part == k_split - 1 else F32)
        return res
    return _matmul_part(x, xs, w, layer, n_out, w_col0, 1, 0, res, rope, out_dtype)


def _matmul_part(x, xs, w, layer, n_out, w_col0, k_split, part, res, rope, out_dtype):
    m, ms = x.shape[0], xs.shape[0]
    k = x.shape[1] // k_split
    assert x.shape[1] % k_split == 0 and k % LANES == 0
    tm, tn, planned = _mm_tiles(m, k, n_out, res is not None, m if rope is None else rope[0][0].shape[0])
    assert w_col0 % tn == 0
    col0 = w_col0 // tn
    row_tile = pl.BlockSpec((tm, tn), lambda j, i: (i, j))
    sample_tile = pl.BlockSpec((ms, tn), lambda j, i: (0, j))
    in_specs = [pl.BlockSpec((tm, k), lambda j, i: (i, part)),
                pl.BlockSpec((ms, k), lambda j, i: (0, part)),
                pl.BlockSpec((None, k, tn), lambda j, i: (layer, part, j + col0))]
    args = [x, xs, w]
    if res is not None:
        in_specs += [row_tile, sample_tile]
        args += list(res)
    if rope is not None:
        (cos, sin_signed), (cos_s, sin_s) = rope
        p_tiles = cos.shape[0] // tm
        assert cos.shape[0] % tm == 0 and tn % LANES == 0 and cos_s.shape[0] == ms
        in_specs += [pl.BlockSpec((tm, LANES), lambda j, i: (i % p_tiles, 0))] * 2
        in_specs += [pl.BlockSpec((ms, LANES), lambda j, i: (0, 0))] * 2
        args += [cos, sin_signed, cos_s, sin_s]
    return pl.pallas_call(
        functools.partial(_mm_kernel, has_res=res is not None, has_rope=rope is not None),
        grid=(n_out // tn, m // tm),
        in_specs=in_specs,
        out_specs=[row_tile, sample_tile],
        out_shape=[jax.ShapeDtypeStruct((m, n_out), out_dtype),
                   jax.ShapeDtypeStruct((ms, n_out), out_dtype)],
        scratch_shapes=[pltpu.VMEM((k, tn), BF16)],
        compiler_params=_params(planned, 2),
        name="matmul",
    )(*args)


def _gla_gate_norm(o, r, gh):
    y = o * lax.rsqrt(jnp.mean(o * o, axis=-1, keepdims=True) + EPS) * gh
    return y * (r * _sigmoid(r))


def _gla_gate_kernel(a_ref, wa_ref, ba_ref, g_ref):
    z = jnp.dot(a_ref[...], wa_ref[...], precision=HIGHEST, preferred_element_type=F32) + ba_ref[...]
    g_ref[...] = _log_sigmoid(z) / GLA_GATE_TEMP


def _gla_gate(a_pad, wa_pad, b_a2):
    m = a_pad.shape[0]
    n = wa_pad.shape[1]
    tm = _pick(m, (512, 256, 128, 64, 32, 16, 8))
    planned = 2 * tm * LANES * 4 + 2 * LANES * n * 4 + 4 * tm * n * 4
    return pl.pallas_call(
        _gla_gate_kernel,
        grid=(m // tm,),
        in_specs=[pl.BlockSpec((tm, LANES), lambda i: (i, 0)),
                  pl.BlockSpec((LANES, n), lambda i: (0, 0)),
                  pl.BlockSpec((1, n), lambda i: (0, 0))],
        out_specs=pl.BlockSpec((tm, n), lambda i: (i, 0)),
        out_shape=jax.ShapeDtypeStruct((m, n), F32),
        compiler_params=_params(planned, 1),
        name="gla_gate",
    )(a_pad, wa_pad, b_a2)


def _gla_prompt_kernel(pt_ref, q_ref, k_ref, v_ref, r_ref, g_ref, gh_ref, *refs, dk, pages_per_block):
    page_refs = refs[:-4]
    o_ref, s_ref, mean_ref, st_ref = refs[-4:]
    c = pl.program_id(2)

    for t in range(len(page_refs) // pages_per_block):
        block_refs = page_refs[t * pages_per_block:(t + 1) * pages_per_block]
        total = jnp.sum(block_refs[0][0], axis=0)
        for ck_ref in block_refs[1:]:
            total = total + jnp.sum(ck_ref[0], axis=0)
        mean_ref[t] = total * (1.0 / (pages_per_block * block_refs[0].shape[1]))

    @pl.when(c == 0)
    def _():
        st_ref[...] = jnp.zeros_like(st_ref)

    n_c = q_ref.shape[0]
    g = g_ref[...]
    row = lax.broadcasted_iota(jnp.int32, (n_c, n_c), 0)
    col = lax.broadcasted_iota(jnp.int32, (n_c, n_c), 1)
    causal = col <= row
    b = jnp.dot(causal.astype(F32), g, precision=HIGHEST, preferred_element_type=F32)
    b_last = b[n_c - 1:n_c, :]

    q = q_ref[...] * (dk ** -0.5)
    k = k_ref[...]
    v = v_ref[...].astype(BF16)
    qe = (q * jnp.exp(b)).astype(BF16)
    kd = (k * jnp.exp(-b)).astype(BF16)
    kr = (k * jnp.exp(b_last - b)).astype(BF16)
    att = jnp.where(causal, _nt(qe, kd), 0.0).astype(BF16)
    st_old = st_ref[...]
    o = jnp.dot(att, v, preferred_element_type=F32) + _nt(qe, st_old.astype(BF16))
    st_new = st_old * jnp.exp(b_last) + _tn(v, kr)
    st_ref[...] = st_new
    o_ref[...] = _gla_gate_norm(o, r_ref[...], gh_ref[...]).astype(o_ref.dtype)

    @pl.when(c == pl.num_programs(2) - 1)
    def _():
        s_ref[0, 0] = st_new.T


def _gla_prompt(proj, gate, g_head, page_table, cache_k, batch, seq, heads, dk, dv):
    assert seq % GLA_CHUNK == 0 and dv == 2 * dk
    n_chunks = seq // GLA_CHUNK
    m = batch * seq
    n_steps = batch * heads * n_chunks
    n_seq, n_pages = page_table.shape
    _, page, att_h, hd = cache_k.shape
    pages_per_block = MOBA_BLOCK // page
    assert n_pages % pages_per_block == 0
    n_blocks = n_pages // pages_per_block
    total_blocks = n_seq * n_blocks
    blocks_per_step = -(-total_blocks // n_steps)
    assert total_blocks % blocks_per_step == 0
    last_group = total_blocks // blocks_per_step - 1

    rows = lambda b, h, c: b * n_chunks + c
    group = lambda b, h, c: jnp.minimum((b * heads + h) * n_chunks + c, last_group)

    def page_spec(t, pp):
        def index(b, h, c, pt):
            blk = group(b, h, c) * blocks_per_step + t
            return (pt[blk // n_blocks, (blk % n_blocks) * pages_per_block + pp], 0, 0, 0)
        return pl.BlockSpec((1, page, att_h, hd), index)

    page_specs = [page_spec(t, pp) for t in range(blocks_per_step) for pp in range(pages_per_block)]
    planned = (2 * (3 * GLA_CHUNK * dk + 3 * GLA_CHUNK * dv + 2 * dk * dv) * 4 + 7 * dk * dv * 4
               + 3 * len(page_specs) * page * att_h * hd * 4)
    grid_spec = pltpu.PrefetchScalarGridSpec(
        num_scalar_prefetch=1,
        grid=(batch, heads, n_chunks),
        in_specs=[
            pl.BlockSpec((GLA_CHUNK, dk), lambda b, h, c, pt: (rows(b, h, c), h)),
            pl.BlockSpec((GLA_CHUNK, dk), lambda b, h, c, pt: (rows(b, h, c), heads + h)),
            pl.BlockSpec((GLA_CHUNK, dv), lambda b, h, c, pt: (rows(b, h, c), heads + h)),
            pl.BlockSpec((GLA_CHUNK, dv), lambda b, h, c, pt: (rows(b, h, c), 2 * heads + h)),
            pl.BlockSpec((GLA_CHUNK, dk), lambda b, h, c, pt: (rows(b, h, c), h)),
            pl.BlockSpec((1, dv), lambda b, h, c, pt: (0, 0)),
        ] + page_specs,
        out_specs=[
            pl.BlockSpec((GLA_CHUNK, dv), lambda b, h, c, pt: (rows(b, h, c), h)),
            pl.BlockSpec((1, 1, dk, dv), lambda b, h, c, pt: (b, h, 0, 0)),
            pl.BlockSpec((blocks_per_step, att_h, hd), lambda b, h, c, pt: (group(b, h, c), 0, 0)),
        ],
        scratch_shapes=[pltpu.VMEM((dv, dk), F32)],
    )
    og, state, means = pl.pallas_call(
        functools.partial(_gla_prompt_kernel, dk=dk, pages_per_block=pages_per_block),
        grid_spec=grid_spec,
        out_shape=[jax.ShapeDtypeStruct((m, heads * dv), BF16),
                   jax.ShapeDtypeStruct((batch, heads, dk, dv), F32),
                   jax.ShapeDtypeStruct((total_blocks, att_h, hd), F32)],
        compiler_params=_params(planned, 3),
        name="gla_prompt",
    )(page_table, proj, proj, proj, proj, gate, g_head, *([cache_k] * len(page_specs)))
    return og, state, means.reshape(n_seq, n_blocks, att_h, hd)


def _gla_sample_kernel(q_ref, k_ref, v_ref, r_ref, g_ref, gh_ref, s0_ref, o_ref, s_ref, *, dk):
    g = g_ref[0]
    q = q_ref[0] * (dk ** -0.5)
    k = k_ref[0]
    stacked = jnp.concatenate([q, k, g, jnp.zeros((SUBLANES - 3, q.shape[1]), F32)], axis=0)
    cols = stacked.T
    qc, kc, gc = cols[:, 0:1], cols[:, 1:2], cols[:, 2:3]
    s_new = jnp.exp(gc) * s0_ref[0, 0, 0] + kc * v_ref[0]
    s_ref[0, 0] = s_new
    o = jnp.sum(qc * s_new, axis=0, keepdims=True)
    o_ref[0] = _gla_gate_norm(o, r_ref[0], gh_ref[...]).astype(o_ref.dtype)


def _gla_sample(proj, gate, g_head, state, layer, heads, dk, dv):
    n_seq = proj.shape[0]
    planned = 2 * 2 * dk * dv * 4 + 4 * dk * dv * 4
    return pl.pallas_call(
        functools.partial(_gla_sample_kernel, dk=dk),
        grid=(n_seq, heads),
        in_specs=[
            pl.BlockSpec((1, 1, dk), lambda b, h: (b, 0, h)),
            pl.BlockSpec((1, 1, dk), lambda b, h: (b, 0, heads + h)),
            pl.BlockSpec((1, 1, dv), lambda b, h: (b, 0, heads + h)),
            pl.BlockSpec((1, 1, dv), lambda b, h: (b, 0, 2 * heads + h)),
            pl.BlockSpec((1, 1, dk), lambda b, h: (b, 0, h)),
            pl.BlockSpec((1, dv), lambda b, h: (0, 0)),
            pl.BlockSpec((1, 1, 1, dk, dv), lambda b, h: (layer, b, h, 0, 0)),
        ],
        out_specs=[
            pl.BlockSpec((1, 1, dv), lambda b, h: (b, 0, h)),
            pl.BlockSpec((1, 1, dk, dv), lambda b, h: (b, h, 0, 0)),
        ],
        out_shape=[jax.ShapeDtypeStruct((n_seq, 1, heads * dv), BF16),
                   jax.ShapeDtypeStruct((n_seq, heads, dk, dv), F32)],
        compiler_params=_params(planned, 2),
        name="gla_sample",
    )(proj, proj, proj, proj, gate, g_head, state)


def _conv_gate(ua, ug, wca, wcg, bca, bcg):
    ca = bca + wca[0:1] * ua[0] + wca[1:2] * ua[1] + wca[2:3] * ua[2]
    cg = bcg + wcg[0:1] * ug[0] + wcg[1:2] * ug[1] + wcg[2:3] * ug[2]
    return ca * _sigmoid(ca) * cg


def _up_kernel(x_ref, xs_ref, wa_ref, wg_ref, wca_ref, wcg_ref, bca_ref, bcg_ref, sa_ref, sg_ref,
               act_ref, st_ref, acts_ref, us_ref, wba_ref, wbg_ref, ua_ref, ug_ref, *, tiles_per_seq):
    i = pl.program_id(1)
    tm = x_ref.shape[0]
    conv = lambda ua, ug: _conv_gate(ua, ug, wca_ref[...], wcg_ref[...], bca_ref[...], bcg_ref[...])

    @pl.when(i == 0)
    def _():
        _cast_weight_tile(wa_ref, wba_ref)
        _cast_weight_tile(wg_ref, wbg_ref)
        xs = xs_ref[...]
        ua = jnp.dot(xs, wba_ref[...], preferred_element_type=F32)
        ug = jnp.dot(xs, wbg_ref[...], preferred_element_type=F32)
        acts_ref[...] = conv((sa_ref[0], sa_ref[1], ua), (sg_ref[0], sg_ref[1], ug)).astype(acts_ref.dtype)
        us_ref[0] = ua
        us_ref[1] = ug

    @pl.when(i % tiles_per_seq == 0)
    def _():
        ua_ref[0:SUBLANES, :] = jnp.zeros((SUBLANES, ua_ref.shape[1]), F32)
        ug_ref[0:SUBLANES, :] = jnp.zeros((SUBLANES, ug_ref.shape[1]), F32)

    x = x_ref[...]
    ua_ref[SUBLANES:, :] = jnp.dot(x, wba_ref[...], preferred_element_type=F32)
    ug_ref[SUBLANES:, :] = jnp.dot(x, wbg_ref[...], preferred_element_type=F32)
    taps = lambda u_ref: tuple(u_ref[pl.ds(SUBLANES - 2 + d, tm), :] for d in range(3))
    act_ref[...] = conv(taps(ua_ref), taps(ug_ref)).astype(act_ref.dtype)
    st_ref[0, 0] = ua_ref[pl.ds(tm + SUBLANES - 2, 2), :]
    st_ref[0, 1] = ug_ref[pl.ds(tm + SUBLANES - 2, 2), :]
    ua_ref[0:SUBLANES, :] = ua_ref[pl.ds(tm, SUBLANES), :]
    ug_ref[0:SUBLANES, :] = ug_ref[pl.ds(tm, SUBLANES), :]


def _up_tiles(m, k, f, seq):
    for tm, tn in ((1024, 256), (512, 256), (512, 128), (256, 128), (128, 128)):
        if seq % tm or f % tn:
            continue
        planned = (2 * 2 * k * tn * 4 + 2 * k * tn * 2 + 2 * tm * k * 2
                   + 2 * (tm + SUBLANES) * tn * 4 + 2 * tm * tn * 2 + 4 * tm * tn * 4)
        if planned <= VMEM_BUDGET_BYTES:
            return tm, tn, planned
    raise ValueError(f"no up-projection tiling for {(m, k, f)}")


def _up(xn, xs, w_up, w_conv, b_conv, conv_rows, layer, batch, seq):
    m, k = xn.shape
    ms = xs.shape[0]
    f = w_up.shape[2] // 2
    assert w_conv.shape[1] == 3
    tm, tn, planned = _up_tiles(m, k, f, seq)
    g0 = f // tn
    tiles_per_seq = seq // tm
    half = lambda block, off: pl.BlockSpec((None,) + block, lambda j, i: (layer,) + (0,) * (len(block) - 1) + (j + off,))
    act, st, act_s, u_s = pl.pallas_call(
        functools.partial(_up_kernel, tiles_per_seq=tiles_per_seq),
        grid=(f // tn, m // tm),
        in_specs=[
            pl.BlockSpec((tm, k), lambda j, i: (i, 0)),
            pl.BlockSpec((ms, k), lambda j, i: (0, 0)),
            half((k, tn), 0), half((k, tn), g0),
            half((3, tn), 0), half((3, tn), g0),
            half((1, tn), 0), half((1, tn), g0),
            half((2, ms, tn), 0), half((2, ms, tn), g0),
        ],
        out_specs=[
            pl.BlockSpec((tm, tn), lambda j, i: (i, j)),
            pl.BlockSpec((1, 2, 2, tn), lambda j, i: (i // tiles_per_seq, 0, 0, j)),
            pl.BlockSpec((ms, tn), lambda j, i: (0, j)),
            pl.BlockSpec((2, ms, tn), lambda j, i: (0, 0, j)),
        ],
        out_shape=[jax.ShapeDtypeStruct((m, f), BF16),
                   jax.ShapeDtypeStruct((batch, 2, 2, f), F32),
                   jax.ShapeDtypeStruct((ms, f), BF16),
                   jax.ShapeDtypeStruct((2, ms, f), F32)],
        scratch_shapes=[pltpu.VMEM((k, tn), BF16), pltpu.VMEM((k, tn), BF16),
                        pltpu.VMEM((tm + SUBLANES, tn), F32), pltpu.VMEM((tm + SUBLANES, tn), F32)],
        compiler_params=_params(planned, 2),
        name="up_conv",
    )(xn, xs, w_up, w_up, w_conv, w_conv, b_conv, b_conv, conv_rows, conv_rows)
    return (act, st.transpose(0, 2, 1, 3).reshape(batch, 2, 2 * f),
            act_s, u_s.transpose(1, 0, 2).reshape(ms, 2 * f))


def _moba_prompt_kernel(q_ref, k_ref, v_ref, o_ref, kb_ref, vt_ref, mean_ref, p_ref,
                        *, n_blocks, scale):
    blk = MOBA_BLOCK
    hd = q_ref.shape[1]
    mean_ref[...] = jnp.zeros_like(mean_ref)
    for j in range(n_blocks):
        rows = slice(j * blk, (j + 1) * blk)
        kf = k_ref[rows, :]
        kb_ref[rows, :] = kf.astype(BF16)
        mean_ref[j:j + 1, :] = jnp.mean(kf, axis=0, keepdims=True)
        vt_ref[0:hd, rows] = v_ref[rows, :].T.astype(BF16)
    vt_ref[hd:2 * hd, :] = jnp.ones((hd, vt_ref.shape[1]), BF16)

    key_l = lax.broadcasted_iota(jnp.int32, (blk, blk), 0)
    qry_l = lax.broadcasted_iota(jnp.int32, (blk, blk), 1)
    causal = key_l <= qry_l
    for i in range(n_blocks):
        n_keys = (i + 1) * blk
        qf = q_ref[i * blk:(i + 1) * blk, :]
        s_t = _nt(kb_ref[0:n_keys, :], (qf * scale).astype(BF16))
        parts = [s_t[j * blk:(j + 1) * blk, :] for j in range(i + 1)]
        parts[i] = jnp.where(causal, parts[i], NEG_BIG)
        if i > MOBA_TOPK:
            gate_t = _nt(mean_ref[...], qf, precision=HIGHEST)
            blk_id = lax.broadcasted_iota(jnp.int32, gate_t.shape, 0)
            for j in range(i):
                gj = gate_t[j:j + 1, :]
                beats = (blk_id < i) & ((gate_t > gj) | ((gate_t == gj) & (blk_id < j)))
                ahead = jnp.sum(jnp.where(beats, 1.0, 0.0), axis=0, keepdims=True)
                parts[j] = jnp.where(ahead < MOBA_TOPK, parts[j], NEG_BIG)
        mx = parts[0].max(axis=0, keepdims=True)
        for part in parts[1:]:
            mx = jnp.maximum(mx, part.max(axis=0, keepdims=True))
        for j, part in enumerate(parts):
            p_ref[j * blk:(j + 1) * blk, :] = jnp.exp(part - mx).astype(BF16)
        o_t = jnp.dot(vt_ref[:, 0:n_keys], p_ref[0:n_keys, :], preferred_element_type=F32)
        o_t = o_t[0:hd, :] / o_t[hd:hd + 1, :]
        o_ref[i * blk:(i + 1) * blk, :] = o_t.T.astype(o_ref.dtype)


def _moba_prompt(q, k, v, batch, seq, heads, hd):
    assert seq % MOBA_BLOCK == 0 and hd == LANES
    n_blocks = seq // MOBA_BLOCK
    mean_rows = -(-n_blocks // SUBLANES) * SUBLANES
    planned = (2 * 3 * seq * hd * 4 + 2 * seq * hd * 2 + seq * hd * 2 + 2 * hd * seq * 2
               + seq * MOBA_BLOCK * 2 + 4 * seq * MOBA_BLOCK * 4)
    head_rows = pl.BlockSpec((seq, hd), lambda b, h: (b, h))
    return pl.pallas_call(
        functools.partial(_moba_prompt_kernel, n_blocks=n_blocks, scale=hd ** -0.5),
        grid=(batch, heads),
        in_specs=[head_rows, head_rows, head_rows],
        out_specs=head_rows,
        out_shape=jax.ShapeDtypeStruct((batch * seq, heads * hd), BF16),
        scratch_shapes=[pltpu.VMEM((seq, hd), BF16), pltpu.VMEM((2 * hd, seq), BF16),
                        pltpu.VMEM((mean_rows, hd), F32), pltpu.VMEM((seq, MOBA_BLOCK), BF16)],
        compiler_params=_params(planned, 2),
        name="moba_prompt",
    )(q, k, v)


def _moba_select_kernel(mean_ref, q_ref, idx_ref, *, n_top):
    n_blocks, heads, hd = mean_ref.shape[1:]
    prod = (mean_ref[0] * q_ref[...]).reshape(n_blocks * heads, hd)
    gate = jnp.dot(prod, jnp.ones((hd, LANES), F32), precision=HIGHEST,
                   preferred_element_type=F32).reshape(n_blocks, heads, LANES)
    blk_id = lax.broadcasted_iota(jnp.int32, gate.shape, 0)
    for t in range(n_top):
        best = gate.max(axis=0)
        first = jnp.where(gate == best[None], blk_id, n_blocks).min(axis=0)
        idx_ref[0, t] = first
        gate = jnp.where(blk_id == first[None], -jnp.inf, gate)


def _moba_select(means, q):
    n_seq, n_blocks, heads, hd = means.shape
    assert heads % SUBLANES == 0
    n_top = min(MOBA_TOPK, n_blocks)
    planned = 6 * n_blocks * heads * hd * 4 + 2 * n_top * heads * LANES * 4
    idx = pl.pallas_call(
        functools.partial(_moba_select_kernel, n_top=n_top),
        grid=(n_seq,),
        in_specs=[pl.BlockSpec((1, n_blocks, heads, hd), lambda b: (b, 0, 0, 0)),
                  pl.BlockSpec((1, heads, hd), lambda b: (b, 0, 0))],
        out_specs=pl.BlockSpec((1, n_top, heads, LANES), lambda b: (b, 0, 0, 0)),
        out_shape=jax.ShapeDtypeStruct((n_seq, n_top, heads, LANES), jnp.int32),
        compiler_params=_params(planned, 1),
        name="moba_select",
    )(means, q)
    return idx[..., 0]


def _moba_sample_kernel(pt_ref, idx_ref, *refs, n_sel, scale):
    k_refs = refs[:n_sel]
    v_refs = refs[n_sel:2 * n_sel]
    q_ref, kn_ref, vn_ref, o_ref = refs[2 * n_sel:]
    page, group, hd = k_refs[0].shape[1:]
    rows = page * group
    mine = lax.broadcasted_iota(jnp.int32, (SUBLANES, rows), 1) % group == pl.program_id(1) % group
    q8 = jnp.broadcast_to(q_ref[0] * scale, (SUBLANES, hd))
    q8b = q8.astype(BF16)
    scores = [jnp.where(mine, _nt(q8b, k_ref[0].reshape(rows, hd).astype(BF16)), NEG_BIG)
              for k_ref in k_refs]
    s_own = jnp.sum(q8 * kn_ref[0], axis=-1, keepdims=True)
    mx = s_own
    for s in scores:
        mx = jnp.maximum(mx, s.max(axis=-1, keepdims=True))
    p_own = jnp.exp(s_own - mx)
    denom = p_own
    acc = p_own * vn_ref[0]
    for s, v_ref in zip(scores, v_refs):
        p = jnp.exp(s - mx)
        denom = denom + jnp.sum(p, axis=-1, keepdims=True)
        acc = acc + jnp.dot(p.astype(BF16), v_ref[0].reshape(rows, hd).astype(BF16),
                            preferred_element_type=F32)
    o_ref[0] = (acc / denom)[0:1].astype(o_ref.dtype)


def _moba_sample(page_table, picks, cache_k, cache_v, q, k_new, v_new):
    n_seq, n_pages = page_table.shape
    _, page, heads, hd = cache_k.shape
    pages_per_block = MOBA_BLOCK // page
    n_top = picks.shape[1]
    n_sel = n_top * pages_per_block

    def page_spec(t, pp):
        return pl.BlockSpec(
            (1, page, SUBLANES, hd),
            lambda b, h, pt, ix: (pt[b, ix[b, t, h] * pages_per_block + pp], 0, h // SUBLANES, 0))

    page_specs = [page_spec(t, pp) for t in range(n_top) for pp in range(pages_per_block)]
    vec_spec = pl.BlockSpec((1, 1, hd), lambda b, h, pt, ix: (b, 0, h))
    planned = 2 * 2 * n_sel * page * SUBLANES * hd * 4 + 16 * page * SUBLANES * hd * 4
    grid_spec = pltpu.PrefetchScalarGridSpec(
        num_scalar_prefetch=2,
        grid=(n_seq, heads),
        in_specs=page_specs + page_specs + [vec_spec, vec_spec, vec_spec],
        out_specs=vec_spec,
    )
    return pl.pallas_call(
        functools.partial(_moba_sample_kernel, n_sel=n_sel, scale=hd ** -0.5),
        grid_spec=grid_spec,
        out_shape=jax.ShapeDtypeStruct((n_seq, 1, heads * hd), BF16),
        compiler_params=_params(planned, 2),
        name="moba_sample",
    )(page_table, picks, *([cache_k] * n_sel), *([cache_v] * n_sel), q, k_new, v_new)


def _rope_tables(pos, hd):
    half = hd // 2
    inv = ROPE_THETA ** (-jnp.arange(half, dtype=F32) / half)
    ang = pos[:, None] * inv[None, :]
    cos, sin = jnp.cos(ang), jnp.sin(ang)
    return jnp.concatenate([cos, cos], axis=-1), jnp.concatenate([-sin, sin], axis=-1)


def kernel(x_prompt, x_sample, cache_k, cache_v, page_table, state_gla, state_conv, g_mix, w_in_a,
           w_a2, b_a2, g_gla_head, w_out_a, g_kv, w_kv, w_q, w_out_b, g_ffn, w_up, w_conv, b_conv,
           w_down, g_final):
    n_b, seq, d = x_prompt.shape
    n_s, t_s, _ = x_sample.shape
    assert t_s == 1
    _, _, gla_h, dk, dv = state_gla.shape
    _, page, att_h, hd = cache_k.shape
    n_pages = page_table.shape[1]
    past_len = n_pages * page
    rank = w_a2.shape[1]
    qk_cols, v_cols = gla_h * dk, gla_h * dv
    proj_cols = 2 * qk_cols + 2 * v_cols
    da = att_h * hd
    assert rank <= LANES and len(w_in_a) == 1 and len(w_q) == 1

    xp = x_prompt.reshape(n_b * seq, d)
    xs = x_sample.reshape(n_s, d)
    rope_p = _rope_tables(jnp.arange(seq, dtype=F32), hd)
    rope_s = tuple(jnp.tile(t, (n_s, 1)) for t in _rope_tables(past_len + jnp.arange(t_s, dtype=F32), hd))

    w_a1_pad = jnp.pad(w_in_a[:, :, proj_cols:], ((0, 0), (0, 0), (0, LANES - rank)))
    w_a2_pad = jnp.pad(w_a2[0], ((0, LANES - rank), (0, 0)))
    conv_rows = state_conv.transpose(0, 2, 1, 3)
    b_conv3 = b_conv[:, None, :]
    down_split = 2 if (w_down.shape[1] // 2) % LANES == 0 else 1

    def norm(h, gains, dtype=BF16):
        return tuple(zip(_rmsnorm(h[0], gains, dtype), _rmsnorm(h[1], gains, dtype)))

    def layer0(h):
        (hn,) = norm(h, g_mix[0:1])
        proj_p, proj_s = _matmul(*hn, w_in_a, 0, proj_cols)
        a_p, a_s = _matmul(*hn, w_a1_pad, 0, LANES)
        gate_p = _gla_gate(a_p, w_a2_pad, b_a2[0:1])
        gate_s = _gla_gate(a_s, w_a2_pad, b_a2[0:1])
        og_p, s_p, key_means = _gla_prompt(proj_p, gate_p, g_gla_head[0:1], page_table, cache_k,
                                           n_b, seq, gla_h, dk, dv)
        og_s, s_s = _gla_sample(proj_s.reshape(n_s, 1, proj_cols), gate_s.reshape(n_s, 1, qk_cols),
                                g_gla_head[0:1], state_gla, 0, gla_h, dk, dv)
        return _matmul(og_p, og_s.reshape(n_s, v_cols), w_out_a, 0, d, res=h), (s_p, s_s), key_means

    def conv_ffn(h, layer):
        (hn,) = norm(h, g_ffn[layer:layer + 1])
        act_p, st_p, act_s, u_s = _up(*hn, w_up, w_conv, b_conv3, conv_rows, layer, n_b, seq)
        st_s = jnp.stack([state_conv[layer][:, 1, :], u_s], axis=1)
        return _matmul(act_p, act_s, w_down, layer, d, k_split=down_split, res=h), (st_p, st_s)

    def layer1(h, key_means):
        hn, kvn = norm(h, jnp.stack([g_mix[1], g_kv]))
        rope = (rope_p, rope_s)
        k = _matmul(*kvn, w_kv[None], 0, da, rope=rope)
        v = _matmul(*kvn, w_kv[None], 0, da, w_col0=da)
        q = _matmul(*hn, w_q, 0, da, rope=rope)
        att_p = _moba_prompt(q[0], k[0], v[0], n_b, seq, att_h, hd)
        q3, k3, v3 = (t[1].reshape(n_s, 1, da) for t in (q, k, v))
        picks = _moba_select(key_means, q[1].reshape(n_s, att_h, hd))
        att_s = _moba_sample(page_table, picks, cache_k, cache_v, q3, k3, v3).reshape(n_s, da)
        return _matmul(att_p, att_s, w_out_b, 0, d, res=h), k, v

    h, (gla_p, gla_s), key_means = layer0((xp, xs))
    h, conv0 = conv_ffn(h, 0)
    h, (k_p, k_s), (v_p, v_s) = layer1(h, key_means)
    h, conv1 = conv_ffn(h, 1)
    ((y_p, y_s),) = norm(h, g_final[None, :], F32)
    gla_p, gla_s = gla_p[None], gla_s[None]
    conv_p, conv_s = jnp.stack([conv0[0], conv1[0]]), jnp.stack([conv0[1], conv1[1]])
    return (y_p.reshape(n_b, seq, d), y_s.reshape(n_s, t_s, d),
            k_p.reshape(n_b, seq, att_h, hd), v_p.reshape(n_b, seq, att_h, hd), gla_p, conv_p,
            k_s.reshape(n_s, t_s, att_h, hd), v_s.reshape(n_s, t_s, att_h, hd), gla_s, conv_s)
```

```python
import functools

import jax
import jax.numpy as jnp
from jax import lax
from jax.experimental import pallas as pl
from jax.experimental.pallas import tpu as pltpu

F32 = jnp.float32
BF16 = jnp.bfloat16
HIGHEST = lax.Precision.HIGHEST

GLA_GATE_TEMP = 16.0
GLA_CHUNK = 64
MOBA_BLOCK = 256
MOBA_TOPK = 3
ROPE_THETA = 10000.0
EPS = 1e-6

LANES = 128
SUBLANES = 8
MXU_WIDTH = 256
VMEM_PHYSICAL_BYTES = 64 * 1024 * 1024
VMEM_BUDGET_BYTES = 47 * 1024 * 1024
VMEM_COMPILER_SLACK_BYTES = 8 * 1024 * 1024
NEG_BIG = -1e30


def _vmem_limit(planned_bytes):
    return int(min(planned_bytes + VMEM_COMPILER_SLACK_BYTES, VMEM_PHYSICAL_BYTES - 4 * 1024 * 1024))


def _params(planned_bytes, n_axes):
    return pltpu.CompilerParams(
        dimension_semantics=("arbitrary",) * n_axes,
        vmem_limit_bytes=_vmem_limit(planned_bytes),
    )


def _nt(a, b, precision=None):
    return lax.dot_general(a, b, (((1,), (1,)), ((), ())), precision=precision,
                           preferred_element_type=F32)


def _tn(a, b, precision=None):
    return lax.dot_general(a, b, (((0,), (0,)), ((), ())), precision=precision,
                           preferred_element_type=F32)


def _sigmoid(x):
    return 1.0 / (1.0 + jnp.exp(-x))


def _log_sigmoid(z):
    return jnp.minimum(z, 0.0) - jnp.log1p(jnp.exp(-jnp.abs(z)))


def _pick(n, candidates):
    for c in candidates:
        if n % c == 0:
            return c
    return n


def _norm_kernel(x_ref, g_ref, *o_refs):
    x = x_ref[...]
    y = x * lax.rsqrt(jnp.mean(x * x, axis=-1, keepdims=True) + EPS)
    for i, o_ref in enumerate(o_refs):
        o_ref[...] = (y * g_ref[i:i + 1, :]).astype(o_ref.dtype)


def _rmsnorm(x, gains, out_dtype):
    m, d = x.shape
    n_g = gains.shape[0]
    tm = _pick(m, (256, 128, 64, 32, 16, 8))
    planned = 2 * tm * d * 4 * (1 + n_g)
    outs = pl.pallas_call(
        _norm_kernel,
        grid=(m // tm,),
        in_specs=[pl.BlockSpec((tm, d), lambda i: (i, 0)),
                  pl.BlockSpec((n_g, d), lambda i: (0, 0))],
        out_specs=[pl.BlockSpec((tm, d), lambda i: (i, 0)) for _ in range(n_g)],
        out_shape=[jax.ShapeDtypeStruct((m, d), out_dtype) for _ in range(n_g)],
        compiler_params=_params(planned, 1),
        name="rmsnorm",
    )(x, gains)
    return outs


def _rope_tile(acc, cos, sin_signed):
    heads = []
    for c in range(acc.shape[1] // LANES):
        xc = acc[:, c * LANES:(c + 1) * LANES]
        heads.append(xc * cos + pltpu.roll(xc, LANES // 2, 1) * sin_signed)
    return heads[0] if len(heads) == 1 else jnp.concatenate(heads, axis=1)


def _cast_weight_tile(w_ref, wb_ref):
    k = w_ref.shape[0]
    ck = _pick(k, (512, 256, 128))

    def body(i, carry):
        r = pl.multiple_of(i * ck, ck)
        wb_ref[pl.ds(r, ck), :] = w_ref[pl.ds(r, ck), :].astype(BF16)
        return carry

    lax.fori_loop(0, k // ck, body, 0)


def _mm_kernel(*refs, has_res, has_rope):
    x_ref, xs_ref, w_ref = refs[:3]
    pos = 3
    res_refs = rope_refs = None
    if has_res:
        res_refs = refs[pos:pos + 2]
        pos += 2
    if has_rope:
        rope_refs = refs[pos:pos + 4]
        pos += 4
    o_ref, os_ref, wb_ref = refs[pos:pos + 3]

    tn = wb_ref.shape[1]
    step = MXU_WIDTH if has_rope and tn % MXU_WIDTH == 0 else tn

    def project(lhs_ref, dst_ref, group):
        for c0 in range(0, tn, step):
            cols = slice(c0, c0 + step)
            acc = jnp.dot(lhs_ref[...], wb_ref[:, cols], preferred_element_type=F32)
            if has_rope:
                acc = _rope_tile(acc, rope_refs[2 * group][...], rope_refs[2 * group + 1][...])
            if has_res:
                acc = acc + res_refs[group][:, cols]
            dst_ref[:, cols] = acc.astype(dst_ref.dtype)

    @pl.when(pl.program_id(1) == 0)
    def _():
        _cast_weight_tile(w_ref, wb_ref)
        project(xs_ref, os_ref, 1)

    project(x_ref, o_ref, 0)


def _mm_tiles(m, k, n, has_res, row_period):
    for tm, tn in ((1024, 512), (512, 512), (512, 256), (256, 256), (256, 128), (128, 128),
                   (64, 128), (32, 128), (16, 128), (8, 128)):
        tm = min(tm, m)
        if m % tm or n % tn or row_period % tm:
            continue
        planned = (2 * k * tn * 4 + k * tn * 2 + 2 * tm * k * 2
                   + (2 + 2 * has_res + 1) * tm * tn * 4)
        if planned <= VMEM_BUDGET_BYTES:
            return tm, tn, planned
    raise ValueError(f"no matmul tiling for {(m, k, n)}")


def _matmul(x, xs, w, layer, n_out, *, w_col0=0, k_split=1, res=None, rope=None, out_dtype=F32):
    if k_split > 1:
        assert rope is None
        for part in range(k_split):
            res = _matmul_part(x, xs, w, layer, n_out, w_col0, k_split, part, res, None,
                               out_dtype if part == k_split - 1 else F32)
        return res
    return _matmul_part(x, xs, w, layer, n_out, w_col0, 1, 0, res, rope, out_dtype)


def _matmul_part(x, xs, w, layer, n_out, w_col0, k_split, part, res, rope, out_dtype):
    m, ms = x.shape[0], xs.shape[0]
    k = x.shape[1] // k_split
    assert x.shape[1] % k_split == 0 and k % LANES == 0
    tm, tn, planned = _mm_tiles(m, k, n_out, res is not None, m if rope is None else rope[0][0].shape[0])
    assert w_col0 % tn == 0
    col0 = w_col0 // tn
    row_tile = pl.BlockSpec((tm, tn), lambda j, i: (i, j))
    sample_tile = pl.BlockSpec((ms, tn), lambda j, i: (0, j))
    in_specs = [pl.BlockSpec((tm, k), lambda j, i: (i, part)),
                pl.BlockSpec((ms, k), lambda j, i: (0, part)),
                pl.BlockSpec((None, k, tn), lambda j, i: (layer, part, j + col0))]
    args = [x, xs, w]
    if res is not None:
        in_specs += [row_tile, sample_tile]
        args += list(res)
    if rope is not None:
        (cos, sin_signed), (cos_s, sin_s) = rope
        p_tiles = cos.shape[0] // tm
        assert cos.shape[0] % tm == 0 and tn % LANES == 0 and cos_s.shape[0] == ms
        in_specs += [pl.BlockSpec((tm, LANES), lambda j, i: (i % p_tiles, 0))] * 2
        in_specs += [pl.BlockSpec((ms, LANES), lambda j, i: (0, 0))] * 2
        args += [cos, sin_signed, cos_s, sin_s]
    return pl.pallas_call(
        functools.partial(_mm_kernel, has_res=res is not None, has_rope=rope is not None),
        grid=(n_out // tn, m // tm),
        in_specs=in_specs,
        out_specs=[row_tile, sample_tile],
        out_shape=[jax.ShapeDtypeStruct((m, n_out), out_dtype),
                   jax.ShapeDtypeStruct((ms, n_out), out_dtype)],
        scratch_shapes=[pltpu.VMEM((k, tn), BF16)],
        compiler_params=_params(planned, 2),
        name="matmul",
    )(*args)


def _gla_gate_norm(o, r, gh):
    y = o * lax.rsqrt(jnp.mean(o * o, axis=-1, keepdims=True) + EPS) * gh
    return y * (r * _sigmoid(r))


def _gla_gate_kernel(a_ref, wa_ref, ba_ref, g_ref):
    z = jnp.dot(a_ref[...], wa_ref[...], precision=HIGHEST, preferred_element_type=F32) + ba_ref[...]
    g_ref[...] = _log_sigmoid(z) / GLA_GATE_TEMP


def _gla_gate(a_pad, wa_pad, b_a2):
    m = a_pad.shape[0]
    n = wa_pad.shape[1]
    tm = _pick(m, (512, 256, 128, 64, 32, 16, 8))
    planned = 2 * tm * LANES * 4 + 2 * LANES * n * 4 + 4 * tm * n * 4
    return pl.pallas_call(
        _gla_gate_kernel,
        grid=(m // tm,),
        in_specs=[pl.BlockSpec((tm, LANES), lambda i: (i, 0)),
                  pl.BlockSpec((LANES, n), lambda i: (0, 0)),
                  pl.BlockSpec((1, n), lambda i: (0, 0))],
        out_specs=pl.BlockSpec((tm, n), lambda i: (i, 0)),
        out_shape=jax.ShapeDtypeStruct((m, n), F32),
        compiler_params=_params(planned, 1),
        name="gla_gate",
    )(a_pad, wa_pad, b_a2)


def _gla_prompt_kernel(pt_ref, q_ref, k_ref, v_ref, r_ref, g_ref, gh_ref, *refs, dk, pages_per_block):
    page_refs = refs[:-4]
    o_ref, s_ref, mean_ref, st_ref = refs[-4:]
    c = pl.program_id(2)

    for t in range(len(page_refs) // pages_per_block):
        block_refs = page_refs[t * pages_per_block:(t + 1) * pages_per_block]
        total = jnp.sum(block_refs[0][0], axis=0)
        for ck_ref in block_refs[1:]:
            total = total + jnp.sum(ck_ref[0], axis=0)
        mean_ref[t] = total * (1.0 / (pages_per_block * block_refs[0].shape[1]))

    @pl.when(c == 0)
    def _():
        st_ref[...] = jnp.zeros_like(st_ref)

    n_c = q_ref.shape[0]
    g = g_ref[...]
    row = lax.broadcasted_iota(jnp.int32, (n_c, n_c), 0)
    col = lax.broadcasted_iota(jnp.int32, (n_c, n_c), 1)
    causal = col <= row
    b = jnp.dot(causal.astype(F32), g, precision=HIGHEST, preferred_element_type=F32)
    b_last = b[n_c - 1:n_c, :]

    q = q_ref[...] * (dk ** -0.5)
    k = k_ref[...]
    v = v_ref[...].astype(BF16)
    qe = (q * jnp.exp(b)).astype(BF16)
    kd = (k * jnp.exp(-b)).astype(BF16)
    kr = (k * jnp.exp(b_last - b)).astype(BF16)
    att = jnp.where(causal, _nt(qe, kd), 0.0).astype(BF16)
    st_old = st_ref[...]
    o = jnp.dot(att, v, preferred_element_type=F32) + _nt(qe, st_old.astype(BF16))
    st_new = st_old * jnp.exp(b_last) + _tn(v, kr)
    st_ref[...] = st_new
    o_ref[...] = _gla_gate_norm(o, r_ref[...], gh_ref[...]).astype(o_ref.dtype)

    @pl.when(c == pl.num_programs(2) - 1)
    def _():
        s_ref[0, 0] = st_new.T


def _gla_prompt(proj, gate, g_head, page_table, cache_k, batch, seq, heads, dk, dv):
    assert seq % GLA_CHUNK == 0 and dv == 2 * dk
    n_chunks = seq // GLA_CHUNK
    m = batch * seq
    n_steps = batch * heads * n_chunks
    n_seq, n_pages = page_table.shape
    _, page, att_h, hd = cache_k.shape
    pages_per_block = MOBA_BLOCK // page
    assert n_pages % pages_per_block == 0
    n_blocks = n_pages // pages_per_block
    total_blocks = n_seq * n_blocks
    blocks_per_step = -(-total_blocks // n_steps)
    assert total_blocks % blocks_per_step == 0
    last_group = total_blocks // blocks_per_step - 1

    rows = lambda b, h, c: b * n_chunks + c
    group = lambda b, h, c: jnp.minimum((b * heads + h) * n_chunks + c, last_group)

    def page_spec(t, pp):
        def index(b, h, c, pt):
            blk = group(b, h, c) * blocks_per_step + t
            return (pt[blk // n_blocks, (blk % n_blocks) * pages_per_block + pp], 0, 0, 0)
        return pl.BlockSpec((1, page, att_h, hd), index)

    page_specs = [page_spec(t, pp) for t in range(blocks_per_step) for pp in range(pages_per_block)]
    planned = (2 * (3 * GLA_CHUNK * dk + 3 * GLA_CHUNK * dv + 2 * dk * dv) * 4 + 7 * dk * dv * 4
               + 3 * len(page_specs) * page * att_h * hd * 4)
    grid_spec = pltpu.PrefetchScalarGridSpec(
        num_scalar_prefetch=1,
        grid=(batch, heads, n_chunks),
        in_specs=[
            pl.BlockSpec((GLA_CHUNK, dk), lambda b, h, c, pt: (rows(b, h, c), h)),
            pl.BlockSpec((GLA_CHUNK, dk), lambda b, h, c, pt: (rows(b, h, c), heads + h)),
            pl.BlockSpec((GLA_CHUNK, dv), lambda b, h, c, pt: (rows(b, h, c), heads + h)),
            pl.BlockSpec((GLA_CHUNK, dv), lambda b, h, c, pt: (rows(b, h, c), 2 * heads + h)),
            pl.BlockSpec((GLA_CHUNK, dk), lambda b, h, c, pt: (rows(b, h, c), h)),
            pl.BlockSpec((1, dv), lambda b, h, c, pt: (0, 0)),
        ] + page_specs,
        out_specs=[
            pl.BlockSpec((GLA_CHUNK, dv), lambda b, h, c, pt: (rows(b, h, c), h)),
            pl.BlockSpec((1, 1, dk, dv), lambda b, h, c, pt: (b, h, 0, 0)),
            pl.BlockSpec((blocks_per_step, att_h, hd), lambda b, h, c, pt: (group(b, h, c), 0, 0)),
        ],
        scratch_shapes=[pltpu.VMEM((dv, dk), F32)],
    )
    og, state, means = pl.pallas_call(
        functools.partial(_gla_prompt_kernel, dk=dk, pages_per_block=pages_per_block),
        grid_spec=grid_spec,
        out_shape=[jax.ShapeDtypeStruct((m, heads * dv), BF16),
                   jax.ShapeDtypeStruct((batch, heads, dk, dv), F32),
                   jax.ShapeDtypeStruct((total_blocks, att_h, hd), F32)],
        compiler_params=_params(planned, 3),
        name="gla_prompt",
    )(page_table, proj, proj, proj, proj, gate, g_head, *([cache_k] * len(page_specs)))
    return og, state, means.reshape(n_seq, n_blocks, att_h, hd)


def _gla_sample_kernel(q_ref, k_ref, v_ref, r_ref, g_ref, gh_ref, s0_ref, o_ref, s_ref, *, dk):
    g = g_ref[0]
    q = q_ref[0] * (dk ** -0.5)
    k = k_ref[0]
    stacked = jnp.concatenate([q, k, g, jnp.zeros((SUBLANES - 3, q.shape[1]), F32)], axis=0)
    cols = stacked.T
    qc, kc, gc = cols[:, 0:1], cols[:, 1:2], cols[:, 2:3]
    s_new = jnp.exp(gc) * s0_ref[0, 0, 0] + kc * v_ref[0]
    s_ref[0, 0] = s_new
    o = jnp.sum(qc * s_new, axis=0, keepdims=True)
    o_ref[0] = _gla_gate_norm(o, r_ref[0], gh_ref[...]).astype(o_ref.dtype)


def _gla_sample(proj, gate, g_head, state, layer, heads, dk, dv):
    n_seq = proj.shape[0]
    planned = 2 * 2 * dk * dv * 4 + 4 * dk * dv * 4
    return pl.pallas_call(
        functools.partial(_gla_sample_kernel, dk=dk),
        grid=(n_seq, heads),
        in_specs=[
            pl.BlockSpec((1, 1, dk), lambda b, h: (b, 0, h)),
            pl.BlockSpec((1, 1, dk), lambda b, h: (b, 0, heads + h)),
            pl.BlockSpec((1, 1, dv), lambda b, h: (b, 0, heads + h)),
            pl.BlockSpec((1, 1, dv), lambda b, h: (b, 0, 2 * heads + h)),
            pl.BlockSpec((1, 1, dk), lambda b, h: (b, 0, h)),
            pl.BlockSpec((1, dv), lambda b, h: (0, 0)),
            pl.BlockSpec((1, 1, 1, dk, dv), lambda b, h: (layer, b, h, 0, 0)),
        ],
        out_specs=[
            pl.BlockSpec((1, 1, dv), lambda b, h: (b, 0, h)),
            pl.BlockSpec((1, 1, dk, dv), lambda b, h: (b, h, 0, 0)),
        ],
        out_shape=[jax.ShapeDtypeStruct((n_seq, 1, heads * dv), BF16),
                   jax.ShapeDtypeStruct((n_seq, heads, dk, dv), F32)],
        compiler_params=_params(planned, 2),
        name="gla_sample",
    )(proj, proj, proj, proj, gate, g_head, state)


def _conv_gate(ua, ug, wca, wcg, bca, bcg):
    ca = bca + wca[0:1] * ua[0] + wca[1:2] * ua[1] + wca[2:3] * ua[2]
    cg = bcg + wcg[0:1] * ug[0] + wcg[1:2] * ug[1] + wcg[2:3] * ug[2]
    return ca * _sigmoid(ca) * cg


def _up_kernel(x_ref, xs_ref, wa_ref, wg_ref, wca_ref, wcg_ref, bca_ref, bcg_ref, sa_ref, sg_ref,
               act_ref, st_ref, acts_ref, us_ref, wba_ref, wbg_ref, ua_ref, ug_ref, *, tiles_per_seq):
    i = pl.program_id(1)
    tm = x_ref.shape[0]
    conv = lambda ua, ug: _conv_gate(ua, ug, wca_ref[...], wcg_ref[...], bca_ref[...], bcg_ref[...])

    @pl.when(i == 0)
    def _():
        _cast_weight_tile(wa_ref, wba_ref)
        _cast_weight_tile(wg_ref, wbg_ref)
        xs = xs_ref[...]
        ua = jnp.dot(xs, wba_ref[...], preferred_element_type=F32)
        ug = jnp.dot(xs, wbg_ref[...], preferred_element_type=F32)
        acts_ref[...] = conv((sa_ref[0], sa_ref[1], ua), (sg_ref[0], sg_ref[1], ug)).astype(acts_ref.dtype)
        us_ref[0] = ua
        us_ref[1] = ug

    @pl.when(i % tiles_per_seq == 0)
    def _():
        ua_ref[0:SUBLANES, :] = jnp.zeros((SUBLANES, ua_ref.shape[1]), F32)
        ug_ref[0:SUBLANES, :] = jnp.zeros((SUBLANES, ug_ref.shape[1]), F32)

    x = x_ref[...]
    ua_ref[SUBLANES:, :] = jnp.dot(x, wba_ref[...], preferred_element_type=F32)
    ug_ref[SUBLANES:, :] = jnp.dot(x, wbg_ref[...], preferred_element_type=F32)
    taps = lambda u_ref: tuple(u_ref[pl.ds(SUBLANES - 2 + d, tm), :] for d in range(3))
    act_ref[...] = conv(taps(ua_ref), taps(ug_ref)).astype(act_ref.dtype)
    st_ref[0, 0] = ua_ref[pl.ds(tm + SUBLANES - 2, 2), :]
    st_ref[0, 1] = ug_ref[pl.ds(tm + SUBLANES - 2, 2), :]
    ua_ref[0:SUBLANES, :] = ua_ref[pl.ds(tm, SUBLANES), :]
    ug_ref[0:SUBLANES, :] = ug_ref[pl.ds(tm, SUBLANES), :]


def _up_tiles(m, k, f, seq):
    for tm, tn in ((1024, 256), (512, 256), (512, 128), (256, 128), (128, 128)):
        if seq % tm or f % tn:
            continue
        planned = (2 * 2 * k * tn * 4 + 2 * k * tn * 2 + 2 * tm * k * 2
                   + 2 * (tm + SUBLANES) * tn * 4 + 2 * tm * tn * 2 + 4 * tm * tn * 4)
        if planned <= VMEM_BUDGET_BYTES:
            return tm, tn, planned
    raise ValueError(f"no up-projection tiling for {(m, k, f)}")


def _up(xn, xs, w_up, w_conv, b_conv, conv_rows, layer, batch, seq):
    m, k = xn.shape
    ms = xs.shape[0]
    f = w_up.shape[2] // 2
    assert w_conv.shape[1] == 3
    tm, tn, planned = _up_tiles(m, k, f, seq)
    g0 = f // tn
    tiles_per_seq = seq // tm
    half = lambda block, off: pl.BlockSpec((None,) + block, lambda j, i: (layer,) + (0,) * (len(block) - 1) + (j + off,))
    act, st, act_s, u_s = pl.pallas_call(
        functools.partial(_up_kernel, tiles_per_seq=tiles_per_seq),
        grid=(f // tn, m // tm),
        in_specs=[
            pl.BlockSpec((tm, k), lambda j, i: (i, 0)),
            pl.BlockSpec((ms, k), lambda j, i: (0, 0)),
            half((k, tn), 0), half((k, tn), g0),
            half((3, tn), 0), half((3, tn), g0),
            half((1, tn), 0), half((1, tn), g0),
            half((2, ms, tn), 0), half((2, ms, tn), g0),
        ],
        out_specs=[
            pl.BlockSpec((tm, tn), lambda j, i: (i, j)),
            pl.BlockSpec((1, 2, 2, tn), lambda j, i: (i // tiles_per_seq, 0, 0, j)),
            pl.BlockSpec((ms, tn), lambda j, i: (0, j)),
            pl.BlockSpec((2, ms, tn), lambda j, i: (0, 0, j)),
        ],
        out_shape=[jax.ShapeDtypeStruct((m, f), BF16),
                   jax.ShapeDtypeStruct((batch, 2, 2, f), F32),
                   jax.ShapeDtypeStruct((ms, f), BF16),
                   jax.ShapeDtypeStruct((2, ms, f), F32)],
        scratch_shapes=[pltpu.VMEM((k, tn), BF16), pltpu.VMEM((k, tn), BF16),
                        pltpu.VMEM((tm + SUBLANES, tn), F32), pltpu.VMEM((tm + SUBLANES, tn), F32)],
        compiler_params=_params(planned, 2),
        name="up_conv",
    )(xn, xs, w_up, w_up, w_conv, w_conv, b_conv, b_conv, conv_rows, conv_rows)
    return (act, st.transpose(0, 2, 1, 3).reshape(batch, 2, 2 * f),
            act_s, u_s.transpose(1, 0, 2).reshape(ms, 2 * f))


def _moba_prompt_attend(q_ref, k_ref, v_ref, o_ref, kb_ref, vt_ref, mean_ref, p_ref,
                        *, n_blocks, scale):
    blk = MOBA_BLOCK
    hd = q_ref.shape[1]
    mean_ref[...] = jnp.zeros_like(mean_ref)
    for j in range(n_blocks):
        rows = slice(j * blk, (j + 1) * blk)
        kf = k_ref[rows, :]
        kb_ref[rows, :] = kf.astype(BF16)
        mean_ref[j:j + 1, :] = jnp.mean(kf, axis=0, keepdims=True)
        vt_ref[0:hd, rows] = v_ref[rows, :].T.astype(BF16)
    vt_ref[hd:2 * hd, :] = jnp.ones((hd, vt_ref.shape[1]), BF16)

    key_l = lax.broadcasted_iota(jnp.int32, (blk, blk), 0)
    qry_l = lax.broadcasted_iota(jnp.int32, (blk, blk), 1)
    causal = key_l <= qry_l
    for i in range(n_blocks):
        n_keys = (i + 1) * blk
        qf = q_ref[i * blk:(i + 1) * blk, :]
        s_t = _nt(kb_ref[0:n_keys, :], (qf * scale).astype(BF16))
        parts = [s_t[j * blk:(j + 1) * blk, :] for j in range(i + 1)]
        parts[i] = jnp.where(causal, parts[i], NEG_BIG)
        if i > MOBA_TOPK:
            gate_t = _nt(mean_ref[...], qf, precision=HIGHEST)
            blk_id = lax.broadcasted_iota(jnp.int32, gate_t.shape, 0)
            for j in range(i):
                gj = gate_t[j:j + 1, :]
                beats = (blk_id < i) & ((gate_t > gj) | ((gate_t == gj) & (blk_id < j)))
                ahead = jnp.sum(jnp.where(beats, 1.0, 0.0), axis=0, keepdims=True)
                parts[j] = jnp.where(ahead < MOBA_TOPK, parts[j], NEG_BIG)
        mx = parts[0].max(axis=0, keepdims=True)
        for part in parts[1:]:
            mx = jnp.maximum(mx, part.max(axis=0, keepdims=True))
        for j, part in enumerate(parts):
            p_ref[j * blk:(j + 1) * blk, :] = jnp.exp(part - mx).astype(BF16)
        o_t = jnp.dot(vt_ref[:, 0:n_keys], p_ref[0:n_keys, :], preferred_element_type=F32)
        o_t = o_t[0:hd, :] / o_t[hd:hd + 1, :]
        o_ref[i * blk:(i + 1) * blk, :] = o_t.T.astype(o_ref.dtype)


def _moba_select_kernel(mean_ref, q_ref, idx_ref, *, n_top):
    n_blocks, heads, hd = mean_ref.shape[1:]
    prod = (mean_ref[0] * q_ref[...]).reshape(n_blocks * heads, hd)
    gate = jnp.dot(prod, jnp.ones((hd, LANES), F32), precision=HIGHEST,
                   preferred_element_type=F32).reshape(n_blocks, heads, LANES)
    blk_id = lax.broadcasted_iota(jnp.int32, gate.shape, 0)
    for t in range(n_top):
        best = gate.max(axis=0)
        first = jnp.where(gate == best[None], blk_id, n_blocks).min(axis=0)
        idx_ref[0, t] = first
        gate = jnp.where(blk_id == first[None], -jnp.inf, gate)


def _moba_select(means, q):
    n_seq, n_blocks, heads, hd = means.shape
    assert heads % SUBLANES == 0
    n_top = min(MOBA_TOPK, n_blocks)
    planned = 6 * n_blocks * heads * hd * 4 + 2 * n_top * heads * LANES * 4
    idx = pl.pallas_call(
        functools.partial(_moba_select_kernel, n_top=n_top),
        grid=(n_seq,),
        in_specs=[pl.BlockSpec((1, n_blocks, heads, hd), lambda b: (b, 0, 0, 0)),
                  pl.BlockSpec((1, heads, hd), lambda b: (b, 0, 0))],
        out_specs=pl.BlockSpec((1, n_top, heads, LANES), lambda b: (b, 0, 0, 0)),
        out_shape=jax.ShapeDtypeStruct((n_seq, n_top, heads, LANES), jnp.int32),
        compiler_params=_params(planned, 1),
        name="moba_select",
    )(means, q)
    return idx[..., 0]


def _moba_sample_attend(k_refs, v_refs, q, k_new, v_new, head, scale):
    page, group, hd = k_refs[0].shape[1:]
    rows = page * group
    mine = lax.broadcasted_iota(jnp.int32, (SUBLANES, rows), 1) % group == head % group
    q8 = jnp.broadcast_to(q * scale, (SUBLANES, hd))
    q8b = q8.astype(BF16)
    scores = [jnp.where(mine, _nt(q8b, k_ref[0].reshape(rows, hd).astype(BF16)), NEG_BIG)
              for k_ref in k_refs]
    s_own = jnp.sum(q8 * k_new, axis=-1, keepdims=True)
    mx = s_own
    for s in scores:
        mx = jnp.maximum(mx, s.max(axis=-1, keepdims=True))
    p_own = jnp.exp(s_own - mx)
    denom = p_own
    acc = p_own * v_new
    for s, v_ref in zip(scores, v_refs):
        p = jnp.exp(s - mx)
        denom = denom + jnp.sum(p, axis=-1, keepdims=True)
        acc = acc + jnp.dot(p.astype(BF16), v_ref[0].reshape(rows, hd).astype(BF16),
                            preferred_element_type=F32)
    return (acc / denom)[0:1]


def _moba_kernel(pt_ref, idx_ref, q_ref, k_ref, v_ref, qs_ref, ks_ref, vs_ref, *refs,
                 n_blocks, scale, n_sel, pairs_per_step, n_groups, heads):
    n_slabs = 2 * n_sel * pairs_per_step
    slab_refs = refs[:n_slabs]
    o_ref, os_ref, kb_ref, vt_ref, mean_ref, p_ref = refs[n_slabs:]
    _moba_prompt_attend(q_ref, k_ref, v_ref, o_ref, kb_ref, vt_ref, mean_ref, p_ref,
                        n_blocks=n_blocks, scale=scale)
    group = jnp.minimum(pl.program_id(0) * pl.num_programs(1) + pl.program_id(1), n_groups - 1)
    for t in range(pairs_per_step):
        pair_refs = slab_refs[2 * n_sel * t:2 * n_sel * (t + 1)]
        head = (group * pairs_per_step + t) % heads
        out = _moba_sample_attend(pair_refs[:n_sel], pair_refs[n_sel:], qs_ref[0, t:t + 1, :],
                                  ks_ref[0, t:t + 1, :], vs_ref[0, t:t + 1, :], head, scale)
        os_ref[0, t:t + 1, :] = out.astype(os_ref.dtype)


def _moba(q, k, v, q_s, k_s, v_s, page_table, picks, cache_k, cache_v, batch, seq):
    n_seq, n_pages = page_table.shape
    _, page, heads, hd = cache_k.shape
    assert seq % MOBA_BLOCK == 0 and hd == LANES and q.shape[1] == heads * hd
    n_blocks = seq // MOBA_BLOCK
    mean_rows = -(-n_blocks // SUBLANES) * SUBLANES
    pages_per_block = MOBA_BLOCK // page
    n_top = picks.shape[1]
    n_sel = n_top * pages_per_block
    n_steps = batch * heads
    n_pairs = n_seq * heads
    pairs_per_step = -(-n_pairs // n_steps)
    assert n_pairs % pairs_per_step == 0
    n_groups = n_pairs // pairs_per_step

    group = lambda b, h: jnp.minimum(b * heads + h, n_groups - 1)

    def slab_spec(t, sel, pp):
        def index(b, h, pt, ix):
            pair = group(b, h) * pairs_per_step + t
            s, hs = pair // heads, pair % heads
            return (pt[s, ix[s, sel, hs] * pages_per_block + pp], 0, hs // SUBLANES, 0)
        return pl.BlockSpec((1, page, SUBLANES, hd), index)

    pair_slabs = lambda t: [slab_spec(t, sel, pp) for sel in range(n_top) for pp in range(pages_per_block)]
    slab_specs = [spec for t in range(pairs_per_step) for spec in pair_slabs(t) + pair_slabs(t)]
    slab_args = [c for t in range(pairs_per_step) for c in [cache_k] * n_sel + [cache_v] * n_sel]
    head_rows = pl.BlockSpec((seq, hd), lambda b, h, pt, ix: (b, h))
    pair_rows = pl.BlockSpec((1, pairs_per_step, hd), lambda b, h, pt, ix: (group(b, h), 0, 0))
    as_pairs = lambda x: x.reshape(n_groups, pairs_per_step, hd)
    planned = (2 * 3 * seq * hd * 4 + 2 * seq * hd * 2 + seq * hd * 2 + 2 * hd * seq * 2
               + seq * MOBA_BLOCK * 2 + 4 * seq * MOBA_BLOCK * 4
               + 2 * len(slab_specs) * page * SUBLANES * hd * 4 + 16 * page * SUBLANES * hd * 4)
    grid_spec = pltpu.PrefetchScalarGridSpec(
        num_scalar_prefetch=2,
        grid=(batch, heads),
        in_specs=[head_rows, head_rows, head_rows, pair_rows, pair_rows, pair_rows] + slab_specs,
        out_specs=[head_rows, pair_rows],
        scratch_shapes=[pltpu.VMEM((seq, hd), BF16), pltpu.VMEM((2 * hd, seq), BF16),
                        pltpu.VMEM((mean_rows, hd), F32), pltpu.VMEM((seq, MOBA_BLOCK), BF16)],
    )
    att, att_s = pl.pallas_call(
        functools.partial(_moba_kernel, n_blocks=n_blocks, scale=hd ** -0.5, n_sel=n_sel,
                          pairs_per_step=pairs_per_step, n_groups=n_groups, heads=heads),
        grid_spec=grid_spec,
        out_shape=[jax.ShapeDtypeStruct((batch * seq, heads * hd), BF16),
                   jax.ShapeDtypeStruct((n_groups, pairs_per_step, hd), BF16)],
        compiler_params=_params(planned, 2),
        name="moba",
    )(page_table, picks, q, k, v, as_pairs(q_s), as_pairs(k_s), as_pairs(v_s), *slab_args)
    return att, att_s.reshape(n_seq, heads * hd)


def _rope_tables(pos, hd):
    half = hd // 2
    inv = ROPE_THETA ** (-jnp.arange(half, dtype=F32) / half)
    ang = pos[:, None] * inv[None, :]
    cos, sin = jnp.cos(ang), jnp.sin(ang)
    return jnp.concatenate([cos, cos], axis=-1), jnp.concatenate([-sin, sin], axis=-1)


def kernel(x_prompt, x_sample, cache_k, cache_v, page_table, state_gla, state_conv, g_mix, w_in_a,
           w_a2, b_a2, g_gla_head, w_out_a, g_kv, w_kv, w_q, w_out_b, g_ffn, w_up, w_conv, b_conv,
           w_down, g_final):
    n_b, seq, d = x_prompt.shape
    n_s, t_s, _ = x_sample.shape
    assert t_s == 1
    _, _, gla_h, dk, dv = state_gla.shape
    _, page, att_h, hd = cache_k.shape
    n_pages = page_table.shape[1]
    past_len = n_pages * page
    rank = w_a2.shape[1]
    qk_cols, v_cols = gla_h * dk, gla_h * dv
    proj_cols = 2 * qk_cols + 2 * v_cols
    da = att_h * hd
    assert rank <= LANES and len(w_in_a) == 1 and len(w_q) == 1

    xp = x_prompt.reshape(n_b * seq, d)
    xs = x_sample.reshape(n_s, d)
    rope_p = _rope_tables(jnp.arange(seq, dtype=F32), hd)
    rope_s = tuple(jnp.tile(t, (n_s, 1)) for t in _rope_tables(past_len + jnp.arange(t_s, dtype=F32), hd))

    w_a1_pad = jnp.pad(w_in_a[:, :, proj_cols:], ((0, 0), (0, 0), (0, LANES - rank)))
    w_a2_pad = jnp.pad(w_a2[0], ((0, LANES - rank), (0, 0)))
    conv_rows = state_conv.transpose(0, 2, 1, 3)
    b_conv3 = b_conv[:, None, :]
    down_split = 2 if (w_down.shape[1] // 2) % LANES == 0 else 1

    def norm(h, gains, dtype=BF16):
        return tuple(zip(_rmsnorm(h[0], gains, dtype), _rmsnorm(h[1], gains, dtype)))

    def layer0(h):
        (hn,) = norm(h, g_mix[0:1])
        proj_p, proj_s = _matmul(*hn, w_in_a, 0, proj_cols)
        a_p, a_s = _matmul(*hn, w_a1_pad, 0, LANES)
        gate_p = _gla_gate(a_p, w_a2_pad, b_a2[0:1])
        gate_s = _gla_gate(a_s, w_a2_pad, b_a2[0:1])
        og_p, s_p, key_means = _gla_prompt(proj_p, gate_p, g_gla_head[0:1], page_table, cache_k,
                                           n_b, seq, gla_h, dk, dv)
        og_s, s_s = _gla_sample(proj_s.reshape(n_s, 1, proj_cols), gate_s.reshape(n_s, 1, qk_cols),
                                g_gla_head[0:1], state_gla, 0, gla_h, dk, dv)
        return _matmul(og_p, og_s.reshape(n_s, v_cols), w_out_a, 0, d, res=h), (s_p, s_s), key_means

    def conv_ffn(h, layer):
        (hn,) = norm(h, g_ffn[layer:layer + 1])
        act_p, st_p, act_s, u_s = _up(*hn, w_up, w_conv, b_conv3, conv_rows, layer, n_b, seq)
        st_s = jnp.stack([state_conv[layer][:, 1, :], u_s], axis=1)
        return _matmul(act_p, act_s, w_down, layer, d, k_split=down_split, res=h), (st_p, st_s)

    def layer1(h, key_means):
        hn, kvn = norm(h, jnp.stack([g_mix[1], g_kv]))
        rope = (rope_p, rope_s)
        k = _matmul(*kvn, w_kv[None], 0, da, rope=rope)
        v = _matmul(*kvn, w_kv[None], 0, da, w_col0=da)
        q = _matmul(*hn, w_q, 0, da, rope=rope)
        picks = _moba_select(key_means, q[1].reshape(n_s, att_h, hd))
        att = _moba(q[0], k[0], v[0], q[1], k[1], v[1], page_table, picks, cache_k, cache_v, n_b, seq)
        return _matmul(*att, w_out_b, 0, d, res=h), k, v

    h, (gla_p, gla_s), key_means = layer0((xp, xs))
    h, conv0 = conv_ffn(h, 0)
    h, (k_p, k_s), (v_p, v_s) = layer1(h, key_means)
    h, conv1 = conv_ffn(h, 1)
    ((y_p, y_s),) = norm(h, g_final[None, :], F32)
    gla_p, gla_s = gla_p[None], gla_s[None]
    conv_p, conv_s = jnp.stack([conv0[0], conv1[0]]), jnp.stack([conv0[1], conv1[1]])
    return (y_p.reshape(n_b, seq, d), y_s.reshape(n_s, t_s, d),
            k_p.reshape(n_b, seq, att_h, hd), v_p.reshape(n_b, seq, att_h, hd), gla_p, conv_p,
            k_s.reshape(n_s, t_s, att_h, hd), v_s.reshape(n_s, t_s, att_h, hd), gla_s, conv_s)
```

```python
import functools

import jax
import jax.numpy as jnp
from jax import lax
from jax.experimental import pallas as pl
from jax.experimental.pallas import tpu as pltpu

F32 = jnp.float32
BF16 = jnp.bfloat16
HIGHEST = lax.Precision.HIGHEST

GLA_GATE_TEMP = 16.0
GLA_CHUNK = 64
MOBA_BLOCK = 256
MOBA_TOPK = 3
PAGE_LOOKAHEAD = 2
ROPE_THETA = 10000.0
EPS = 1e-6

LANES = 128
SUBLANES = 8
MXU_WIDTH = 256
VMEM_PHYSICAL_BYTES = 64 * 1024 * 1024
VMEM_BUDGET_BYTES = 47 * 1024 * 1024
VMEM_COMPILER_SLACK_BYTES = 8 * 1024 * 1024
NEG_BIG = -1e30


def _vmem_limit(planned_bytes):
    return int(min(planned_bytes + VMEM_COMPILER_SLACK_BYTES, VMEM_PHYSICAL_BYTES - 4 * 1024 * 1024))


def _params(planned_bytes, n_axes):
    return pltpu.CompilerParams(
        dimension_semantics=("arbitrary",) * n_axes,
        vmem_limit_bytes=_vmem_limit(planned_bytes),
    )


def _nt(a, b, precision=None):
    return lax.dot_general(a, b, (((1,), (1,)), ((), ())), precision=precision,
                           preferred_element_type=F32)


def _tn(a, b, precision=None):
    return lax.dot_general(a, b, (((0,), (0,)), ((), ())), precision=precision,
                           preferred_element_type=F32)


def _sigmoid(x):
    return 1.0 / (1.0 + jnp.exp(-x))


def _log_sigmoid(z):
    return jnp.minimum(z, 0.0) - jnp.log1p(jnp.exp(-jnp.abs(z)))


def _pick(n, candidates):
    for c in candidates:
        if n % c == 0:
            return c
    return n


def _norm_kernel(x_ref, g_ref, *o_refs):
    x = x_ref[...]
    y = x * lax.rsqrt(jnp.mean(x * x, axis=-1, keepdims=True) + EPS)
    for i, o_ref in enumerate(o_refs):
        o_ref[...] = (y * g_ref[i:i + 1, :]).astype(o_ref.dtype)


def _rmsnorm(x, gains, out_dtype):
    m, d = x.shape
    n_g = gains.shape[0]
    tm = _pick(m, (256, 128, 64, 32, 16, 8))
    planned = 2 * tm * d * 4 * (1 + n_g)
    outs = pl.pallas_call(
        _norm_kernel,
        grid=(m // tm,),
        in_specs=[pl.BlockSpec((tm, d), lambda i: (i, 0)),
                  pl.BlockSpec((n_g, d), lambda i: (0, 0))],
        out_specs=[pl.BlockSpec((tm, d), lambda i: (i, 0)) for _ in range(n_g)],
        out_shape=[jax.ShapeDtypeStruct((m, d), out_dtype) for _ in range(n_g)],
        compiler_params=_params(planned, 1),
        name="rmsnorm",
    )(x, gains)
    return outs


def _rope_tile(acc, cos, sin_signed):
    heads = []
    for c in range(acc.shape[1] // LANES):
        xc = acc[:, c * LANES:(c + 1) * LANES]
        heads.append(xc * cos + pltpu.roll(xc, LANES // 2, 1) * sin_signed)
    return heads[0] if len(heads) == 1 else jnp.concatenate(heads, axis=1)


def _cast_weight_tile(w_ref, wb_ref):
    k = w_ref.shape[0]
    ck = _pick(k, (512, 256, 128))

    def body(i, carry):
        r = pl.multiple_of(i * ck, ck)
        wb_ref[pl.ds(r, ck), :] = w_ref[pl.ds(r, ck), :].astype(BF16)
        return carry

    lax.fori_loop(0, k // ck, body, 0)


def _mm_kernel(*refs, has_res, has_rope):
    x_ref, xs_ref, w_ref = refs[:3]
    pos = 3
    res_refs = rope_refs = None
    if has_res:
        res_refs = refs[pos:pos + 2]
        pos += 2
    if has_rope:
        rope_refs = refs[pos:pos + 4]
        pos += 4
    o_ref, os_ref, wb_ref = refs[pos:pos + 3]

    tn = wb_ref.shape[1]
    step = MXU_WIDTH if has_rope and tn % MXU_WIDTH == 0 else tn

    def project(lhs_ref, dst_ref, group):
        for c0 in range(0, tn, step):
            cols = slice(c0, c0 + step)
            acc = jnp.dot(lhs_ref[...], wb_ref[:, cols], preferred_element_type=F32)
            if has_rope:
                acc = _rope_tile(acc, rope_refs[2 * group][...], rope_refs[2 * group + 1][...])
            if has_res:
                acc = acc + res_refs[group][:, cols]
            dst_ref[:, cols] = acc.astype(dst_ref.dtype)

    @pl.when(pl.program_id(1) == 0)
    def _():
        _cast_weight_tile(w_ref, wb_ref)
        project(xs_ref, os_ref, 1)

    project(x_ref, o_ref, 0)


def _mm_tiles(m, k, n, has_res, row_period):
    for tm, tn in ((1024, 512), (512, 512), (512, 256), (256, 256), (256, 128), (128, 128),
                   (64, 128), (32, 128), (16, 128), (8, 128)):
        tm = min(tm, m)
        if m % tm or n % tn or row_period % tm:
            continue
        planned = (2 * k * tn * 4 + k * tn * 2 + 2 * tm * k * 2
                   + (2 + 2 * has_res + 1) * tm * tn * 4)
        if planned <= VMEM_BUDGET_BYTES:
            return tm, tn, planned
    raise ValueError(f"no matmul tiling for {(m, k, n)}")


def _matmul(x, xs, w, layer, n_out, *, w_col0=0, k_split=1, res=None, rope=None, out_dtype=F32):
    if k_split > 1:
        assert rope is None
        for part in range(k_split):
            res = _matmul_part(x, xs, w, layer, n_out, w_col0, k_split, part, res, None,
                               out_dtype if part == k_split - 1 else F32)
        return res
    return _matmul_part(x, xs, w, layer, n_out, w_col0, 1, 0, res, rope, out_dtype)


def _matmul_part(x, xs, w, layer, n_out, w_col0, k_split, part, res, rope, out_dtype):
    m, ms = x.shape[0], xs.shape[0]
    k = x.shape[1] // k_split
    assert x.shape[1] % k_split == 0 and k % LANES == 0
    tm, tn, planned = _mm_tiles(m, k, n_out, res is not None, m if rope is None else rope[0][0].shape[0])
    assert w_col0 % tn == 0
    col0 = w_col0 // tn
    row_tile = pl.BlockSpec((tm, tn), lambda j, i: (i, j))
    sample_tile = pl.BlockSpec((ms, tn), lambda j, i: (0, j))
    in_specs = [pl.BlockSpec((tm, k), lambda j, i: (i, part)),
                pl.BlockSpec((ms, k), lambda j, i: (0, part)),
                pl.BlockSpec((None, k, tn), lambda j, i: (layer, part, j + col0))]
    args = [x, xs, w]
    if res is not None:
        in_specs += [row_tile, sample_tile]
        args += list(res)
    if rope is not None:
        (cos, sin_signed), (cos_s, sin_s) = rope
        p_tiles = cos.shape[0] // tm
        assert cos.shape[0] % tm == 0 and tn % LANES == 0 and cos_s.shape[0] == ms
        in_specs += [pl.BlockSpec((tm, LANES), lambda j, i: (i % p_tiles, 0))] * 2
        in_specs += [pl.BlockSpec((ms, LANES), lambda j, i: (0, 0))] * 2
        args += [cos, sin_signed, cos_s, sin_s]
    return pl.pallas_call(
        functools.partial(_mm_kernel, has_res=res is not None, has_rope=rope is not None),
        grid=(n_out // tn, m // tm),
        in_specs=in_specs,
        out_specs=[row_tile, sample_tile],
        out_shape=[jax.ShapeDtypeStruct((m, n_out), out_dtype),
                   jax.ShapeDtypeStruct((ms, n_out), out_dtype)],
        scratch_shapes=[pltpu.VMEM((k, tn), BF16)],
        compiler_params=_params(planned, 2),
        name="matmul",
    )(*args)


def _gla_gate_norm(o, r, gh):
    y = o * lax.rsqrt(jnp.mean(o * o, axis=-1, keepdims=True) + EPS) * gh
    return y * (r * _sigmoid(r))


def _gla_gate_kernel(a_ref, wa_ref, ba_ref, g_ref):
    z = jnp.dot(a_ref[...], wa_ref[...], precision=HIGHEST, preferred_element_type=F32) + ba_ref[...]
    g_ref[...] = _log_sigmoid(z) / GLA_GATE_TEMP


def _gla_gate(a_pad, wa_pad, b_a2):
    m = a_pad.shape[0]
    n = wa_pad.shape[1]
    tm = _pick(m, (512, 256, 128, 64, 32, 16, 8))
    planned = 2 * tm * LANES * 4 + 2 * LANES * n * 4 + 4 * tm * n * 4
    return pl.pallas_call(
        _gla_gate_kernel,
        grid=(m // tm,),
        in_specs=[pl.BlockSpec((tm, LANES), lambda i: (i, 0)),
                  pl.BlockSpec((LANES, n), lambda i: (0, 0)),
                  pl.BlockSpec((1, n), lambda i: (0, 0))],
        out_specs=pl.BlockSpec((tm, n), lambda i: (i, 0)),
        out_shape=jax.ShapeDtypeStruct((m, n), F32),
        compiler_params=_params(planned, 1),
        name="gla_gate",
    )(a_pad, wa_pad, b_a2)


def _gla_prompt_kernel(pt_ref, q_ref, k_ref, v_ref, r_ref, g_ref, gh_ref, ck_hbm, o_ref, s_ref,
                       mean_ref, st_ref, page_buf, page_sem, *, dk, pages_per_block, n_blocks,
                       n_groups):
    c = pl.program_id(2)
    step = (pl.program_id(0) * pl.num_programs(1) + pl.program_id(1)) * pl.num_programs(2) + c

    pages_per_step = page_buf.shape[1]

    def page_copy(s, j):
        blk = s * (pages_per_step // pages_per_block) + j // pages_per_block
        page_id = pt_ref[blk // n_blocks, (blk % n_blocks) * pages_per_block + j % pages_per_block]
        slot = s % page_buf.shape[0]
        return pltpu.make_async_copy(ck_hbm.at[page_id], page_buf.at[slot, j], page_sem.at[slot, j])

    def start_pages(s):
        for j in range(pages_per_step):
            page_copy(s, j).start()

    @pl.when(step == 0)
    def _():
        for s in range(min(PAGE_LOOKAHEAD, n_groups)):
            start_pages(s)

    @pl.when(step + PAGE_LOOKAHEAD < n_groups)
    def _():
        start_pages(step + PAGE_LOOKAHEAD)

    @pl.when(step < n_groups)
    def _():
        for j in range(pages_per_step):
            page_copy(step, j).wait()
        slot = step % page_buf.shape[0]
        for t in range(pages_per_step // pages_per_block):
            total = jnp.sum(page_buf[slot, t * pages_per_block], axis=0)
            for pp in range(1, pages_per_block):
                total = total + jnp.sum(page_buf[slot, t * pages_per_block + pp], axis=0)
            mean_ref[t] = total * (1.0 / (pages_per_block * page_buf.shape[2]))

    @pl.when(c == 0)
    def _():
        st_ref[...] = jnp.zeros_like(st_ref)

    n_c = q_ref.shape[0]
    g = g_ref[...]
    row = lax.broadcasted_iota(jnp.int32, (n_c, n_c), 0)
    col = lax.broadcasted_iota(jnp.int32, (n_c, n_c), 1)
    causal = col <= row
    b = jnp.dot(causal.astype(F32), g, precision=HIGHEST, preferred_element_type=F32)
    b_last = b[n_c - 1:n_c, :]

    q = q_ref[...] * (dk ** -0.5)
    k = k_ref[...]
    v = v_ref[...].astype(BF16)
    qe = (q * jnp.exp(b)).astype(BF16)
    kd = (k * jnp.exp(-b)).astype(BF16)
    kr = (k * jnp.exp(b_last - b)).astype(BF16)
    att = jnp.where(causal, _nt(qe, kd), 0.0).astype(BF16)
    st_old = st_ref[...]
    o = jnp.dot(att, v, preferred_element_type=F32) + _nt(qe, st_old.astype(BF16))
    st_new = st_old * jnp.exp(b_last) + _tn(v, kr)
    st_ref[...] = st_new
    o_ref[...] = _gla_gate_norm(o, r_ref[...], gh_ref[...]).astype(o_ref.dtype)

    @pl.when(c == pl.num_programs(2) - 1)
    def _():
        s_ref[0, 0] = st_new.T


def _gla_prompt(proj, gate, g_head, page_table, cache_k, batch, seq, heads, dk, dv):
    assert seq % GLA_CHUNK == 0 and dv == 2 * dk
    n_chunks = seq // GLA_CHUNK
    m = batch * seq
    n_steps = batch * heads * n_chunks
    n_seq, n_pages = page_table.shape
    _, page, att_h, hd = cache_k.shape
    pages_per_block = MOBA_BLOCK // page
    assert n_pages % pages_per_block == 0
    n_blocks = n_pages // pages_per_block
    total_blocks = n_seq * n_blocks
    blocks_per_step = -(-total_blocks // n_steps)
    assert total_blocks % blocks_per_step == 0
    n_groups = total_blocks // blocks_per_step
    pages_per_step = blocks_per_step * pages_per_block

    rows = lambda b, h, c: b * n_chunks + c
    group = lambda b, h, c: jnp.minimum((b * heads + h) * n_chunks + c, n_groups - 1)

    planned = (2 * (3 * GLA_CHUNK * dk + 3 * GLA_CHUNK * dv + 2 * dk * dv) * 4 + 7 * dk * dv * 4
               + (PAGE_LOOKAHEAD + 2) * pages_per_step * page * att_h * hd * 4)
    grid_spec = pltpu.PrefetchScalarGridSpec(
        num_scalar_prefetch=1,
        grid=(batch, heads, n_chunks),
        in_specs=[
            pl.BlockSpec((GLA_CHUNK, dk), lambda b, h, c, pt: (rows(b, h, c), h)),
            pl.BlockSpec((GLA_CHUNK, dk), lambda b, h, c, pt: (rows(b, h, c), heads + h)),
            pl.BlockSpec((GLA_CHUNK, dv), lambda b, h, c, pt: (rows(b, h, c), heads + h)),
            pl.BlockSpec((GLA_CHUNK, dv), lambda b, h, c, pt: (rows(b, h, c), 2 * heads + h)),
            pl.BlockSpec((GLA_CHUNK, dk), lambda b, h, c, pt: (rows(b, h, c), h)),
            pl.BlockSpec((1, dv), lambda b, h, c, pt: (0, 0)),
            pl.BlockSpec(memory_space=pl.ANY),
        ],
        out_specs=[
            pl.BlockSpec((GLA_CHUNK, dv), lambda b, h, c, pt: (rows(b, h, c), h)),
            pl.BlockSpec((1, 1, dk, dv), lambda b, h, c, pt: (b, h, 0, 0)),
            pl.BlockSpec((blocks_per_step, att_h, hd), lambda b, h, c, pt: (group(b, h, c), 0, 0)),
        ],
        scratch_shapes=[pltpu.VMEM((dv, dk), F32),
                        pltpu.VMEM((PAGE_LOOKAHEAD + 1, pages_per_step, page, att_h, hd), F32),
                        pltpu.SemaphoreType.DMA((PAGE_LOOKAHEAD + 1, pages_per_step))],
    )
    og, state, means = pl.pallas_call(
        functools.partial(_gla_prompt_kernel, dk=dk, pages_per_block=pages_per_block,
                          n_blocks=n_blocks, n_groups=n_groups),
        grid_spec=grid_spec,
        out_shape=[jax.ShapeDtypeStruct((m, heads * dv), BF16),
                   jax.ShapeDtypeStruct((batch, heads, dk, dv), F32),
                   jax.ShapeDtypeStruct((total_blocks, att_h, hd), F32)],
        compiler_params=_params(planned, 3),
        name="gla_prompt",
    )(page_table, proj, proj, proj, proj, gate, g_head, cache_k)
    return og, state, means.reshape(n_seq, n_blocks, att_h, hd)


def _gla_sample_kernel(q_ref, k_ref, v_ref, r_ref, g_ref, gh_ref, s0_ref, o_ref, s_ref, *, dk):
    g = g_ref[0]
    q = q_ref[0] * (dk ** -0.5)
    k = k_ref[0]
    stacked = jnp.concatenate([q, k, g, jnp.zeros((SUBLANES - 3, q.shape[1]), F32)], axis=0)
    cols = stacked.T
    qc, kc, gc = cols[:, 0:1], cols[:, 1:2], cols[:, 2:3]
    s_new = jnp.exp(gc) * s0_ref[0, 0, 0] + kc * v_ref[0]
    s_ref[0, 0] = s_new
    o = jnp.sum(qc * s_new, axis=0, keepdims=True)
    o_ref[0] = _gla_gate_norm(o, r_ref[0], gh_ref[...]).astype(o_ref.dtype)


def _gla_sample(proj, gate, g_head, state, layer, heads, dk, dv):
    n_seq = proj.shape[0]
    planned = 2 * 2 * dk * dv * 4 + 4 * dk * dv * 4
    return pl.pallas_call(
        functools.partial(_gla_sample_kernel, dk=dk),
        grid=(n_seq, heads),
        in_specs=[
            pl.BlockSpec((1, 1, dk), lambda b, h: (b, 0, h)),
            pl.BlockSpec((1, 1, dk), lambda b, h: (b, 0, heads + h)),
            pl.BlockSpec((1, 1, dv), lambda b, h: (b, 0, heads + h)),
            pl.BlockSpec((1, 1, dv), lambda b, h: (b, 0, 2 * heads + h)),
            pl.BlockSpec((1, 1, dk), lambda b, h: (b, 0, h)),
            pl.BlockSpec((1, dv), lambda b, h: (0, 0)),
            pl.BlockSpec((1, 1, 1, dk, dv), lambda b, h: (layer, b, h, 0, 0)),
        ],
        out_specs=[
            pl.BlockSpec((1, 1, dv), lambda b, h: (b, 0, h)),
            pl.BlockSpec((1, 1, dk, dv), lambda b, h: (b, h, 0, 0)),
        ],
        out_shape=[jax.ShapeDtypeStruct((n_seq, 1, heads * dv), BF16),
                   jax.ShapeDtypeStruct((n_seq, heads, dk, dv), F32)],
        compiler_params=_params(planned, 2),
        name="gla_sample",
    )(proj, proj, proj, proj, gate, g_head, state)


def _conv_gate(ua, ug, wca, wcg, bca, bcg):
    ca = bca + wca[0:1] * ua[0] + wca[1:2] * ua[1] + wca[2:3] * ua[2]
    cg = bcg + wcg[0:1] * ug[0] + wcg[1:2] * ug[1] + wcg[2:3] * ug[2]
    return ca * _sigmoid(ca) * cg


def _up_kernel(x_ref, xs_ref, wa_ref, wg_ref, wca_ref, wcg_ref, bca_ref, bcg_ref, sa_ref, sg_ref,
               act_ref, st_ref, acts_ref, us_ref, wba_ref, wbg_ref, ua_ref, ug_ref, *, tiles_per_seq):
    i = pl.program_id(1)
    tm = x_ref.shape[0]
    conv = lambda ua, ug: _conv_gate(ua, ug, wca_ref[...], wcg_ref[...], bca_ref[...], bcg_ref[...])

    @pl.when(i == 0)
    def _():
        _cast_weight_tile(wa_ref, wba_ref)
        _cast_weight_tile(wg_ref, wbg_ref)
        xs = xs_ref[...]
        ua = jnp.dot(xs, wba_ref[...], preferred_element_type=F32)
        ug = jnp.dot(xs, wbg_ref[...], preferred_element_type=F32)
        acts_ref[...] = conv((sa_ref[0], sa_ref[1], ua), (sg_ref[0], sg_ref[1], ug)).astype(acts_ref.dtype)
        us_ref[0] = ua
        us_ref[1] = ug

    @pl.when(i % tiles_per_seq == 0)
    def _():
        ua_ref[0:SUBLANES, :] = jnp.zeros((SUBLANES, ua_ref.shape[1]), F32)
        ug_ref[0:SUBLANES, :] = jnp.zeros((SUBLANES, ug_ref.shape[1]), F32)

    x = x_ref[...]
    ua_ref[SUBLANES:, :] = jnp.dot(x, wba_ref[...], preferred_element_type=F32)
    ug_ref[SUBLANES:, :] = jnp.dot(x, wbg_ref[...], preferred_element_type=F32)
    taps = lambda u_ref: tuple(u_ref[pl.ds(SUBLANES - 2 + d, tm), :] for d in range(3))
    act_ref[...] = conv(taps(ua_ref), taps(ug_ref)).astype(act_ref.dtype)
    st_ref[0, 0] = ua_ref[pl.ds(tm + SUBLANES - 2, 2), :]
    st_ref[0, 1] = ug_ref[pl.ds(tm + SUBLANES - 2, 2), :]
    ua_ref[0:SUBLANES, :] = ua_ref[pl.ds(tm, SUBLANES), :]
    ug_ref[0:SUBLANES, :] = ug_ref[pl.ds(tm, SUBLANES), :]


def _up_tiles(m, k, f, seq):
    for tm, tn in ((1024, 256), (512, 256), (512, 128), (256, 128), (128, 128)):
        if seq % tm or f % tn:
            continue
        planned = (2 * 2 * k * tn * 4 + 2 * k * tn * 2 + 2 * tm * k * 2
                   + 2 * (tm + SUBLANES) * tn * 4 + 2 * tm * tn * 2 + 4 * tm * tn * 4)
        if planned <= VMEM_BUDGET_BYTES:
            return tm, tn, planned
    raise ValueError(f"no up-projection tiling for {(m, k, f)}")


def _up(xn, xs, w_up, w_conv, b_conv, conv_rows, layer, batch, seq):
    m, k = xn.shape
    ms = xs.shape[0]
    f = w_up.shape[2] // 2
    assert w_conv.shape[1] == 3
    tm, tn, planned = _up_tiles(m, k, f, seq)
    g0 = f // tn
    tiles_per_seq = seq // tm
    half = lambda block, off: pl.BlockSpec((None,) + block, lambda j, i: (layer,) + (0,) * (len(block) - 1) + (j + off,))
    act, st, act_s, u_s = pl.pallas_call(
        functools.partial(_up_kernel, tiles_per_seq=tiles_per_seq),
        grid=(f // tn, m // tm),
        in_specs=[
            pl.BlockSpec((tm, k), lambda j, i: (i, 0)),
            pl.BlockSpec((ms, k), lambda j, i: (0, 0)),
            half((k, tn), 0), half((k, tn), g0),
            half((3, tn), 0), half((3, tn), g0),
            half((1, tn), 0), half((1, tn), g0),
            half((2, ms, tn), 0), half((2, ms, tn), g0),
        ],
        out_specs=[
            pl.BlockSpec((tm, tn), lambda j, i: (i, j)),
            pl.BlockSpec((1, 2, 2, tn), lambda j, i: (i // tiles_per_seq, 0, 0, j)),
            pl.BlockSpec((ms, tn), lambda j, i: (0, j)),
            pl.BlockSpec((2, ms, tn), lambda j, i: (0, 0, j)),
        ],
        out_shape=[jax.ShapeDtypeStruct((m, f), BF16),
                   jax.ShapeDtypeStruct((batch, 2, 2, f), F32),
                   jax.ShapeDtypeStruct((ms, f), BF16),
                   jax.ShapeDtypeStruct((2, ms, f), F32)],
        scratch_shapes=[pltpu.VMEM((k, tn), BF16), pltpu.VMEM((k, tn), BF16),
                        pltpu.VMEM((tm + SUBLANES, tn), F32), pltpu.VMEM((tm + SUBLANES, tn), F32)],
        compiler_params=_params(planned, 2),
        name="up_conv",
    )(xn, xs, w_up, w_up, w_conv, w_conv, b_conv, b_conv, conv_rows, conv_rows)
    return (act, st.transpose(0, 2, 1, 3).reshape(batch, 2, 2 * f),
            act_s, u_s.transpose(1, 0, 2).reshape(ms, 2 * f))


def _moba_prompt_attend(q_ref, k_ref, v_ref, o_ref, kb_ref, vt_ref, mean_ref, p_ref,
                        *, n_blocks, scale):
    blk = MOBA_BLOCK
    hd = q_ref.shape[1]
    mean_ref[...] = jnp.zeros_like(mean_ref)
    for j in range(n_blocks):
        rows = slice(j * blk, (j + 1) * blk)
        kf = k_ref[rows, :]
        kb_ref[rows, :] = kf.astype(BF16)
        mean_ref[j:j + 1, :] = jnp.mean(kf, axis=0, keepdims=True)
        vt_ref[0:hd, rows] = v_ref[rows, :].T.astype(BF16)
    vt_ref[hd:2 * hd, :] = jnp.ones((hd, vt_ref.shape[1]), BF16)

    key_l = lax.broadcasted_iota(jnp.int32, (blk, blk), 0)
    qry_l = lax.broadcasted_iota(jnp.int32, (blk, blk), 1)
    causal = key_l <= qry_l
    for i in range(n_blocks):
        n_keys = (i + 1) * blk
        qf = q_ref[i * blk:(i + 1) * blk, :]
        s_t = _nt(kb_ref[0:n_keys, :], (qf * scale).astype(BF16))
        parts = [s_t[j * blk:(j + 1) * blk, :] for j in range(i + 1)]
        parts[i] = jnp.where(causal, parts[i], NEG_BIG)
        if i > MOBA_TOPK:
            gate_t = _nt(mean_ref[...], qf, precision=HIGHEST)
            blk_id = lax.broadcasted_iota(jnp.int32, gate_t.shape, 0)
            for j in range(i):
                gj = gate_t[j:j + 1, :]
                beats = (blk_id < i) & ((gate_t > gj) | ((gate_t == gj) & (blk_id < j)))
                ahead = jnp.sum(jnp.where(beats, 1.0, 0.0), axis=0, keepdims=True)
                parts[j] = jnp.where(ahead < MOBA_TOPK, parts[j], NEG_BIG)
        mx = parts[0].max(axis=0, keepdims=True)
        for part in parts[1:]:
            mx = jnp.maximum(mx, part.max(axis=0, keepdims=True))
        for j, part in enumerate(parts):
            p_ref[j * blk:(j + 1) * blk, :] = jnp.exp(part - mx).astype(BF16)
        o_t = jnp.dot(vt_ref[:, 0:n_keys], p_ref[0:n_keys, :], preferred_element_type=F32)
        o_t = o_t[0:hd, :] / o_t[hd:hd + 1, :]
        o_ref[i * blk:(i + 1) * blk, :] = o_t.T.astype(o_ref.dtype)


def _moba_select_kernel(mean_ref, q_ref, idx_ref, *, n_top):
    n_blocks, heads, hd = mean_ref.shape[1:]
    prod = (mean_ref[0] * q_ref[...]).reshape(n_blocks * heads, hd)
    gate = jnp.dot(prod, jnp.ones((hd, LANES), F32), precision=HIGHEST,
                   preferred_element_type=F32).reshape(n_blocks, heads, LANES)
    blk_id = lax.broadcasted_iota(jnp.int32, gate.shape, 0)
    for t in range(n_top):
        best = gate.max(axis=0)
        first = jnp.where(gate == best[None], blk_id, n_blocks).min(axis=0)
        idx_ref[0, t] = first
        gate = jnp.where(blk_id == first[None], -jnp.inf, gate)


def _moba_select(means, q):
    n_seq, n_blocks, heads, hd = means.shape
    assert heads % SUBLANES == 0
    n_top = min(MOBA_TOPK, n_blocks)
    planned = 6 * n_blocks * heads * hd * 4 + 2 * n_top * heads * LANES * 4
    idx = pl.pallas_call(
        functools.partial(_moba_select_kernel, n_top=n_top),
        grid=(n_seq,),
        in_specs=[pl.BlockSpec((1, n_blocks, heads, hd), lambda b: (b, 0, 0, 0)),
                  pl.BlockSpec((1, heads, hd), lambda b: (b, 0, 0))],
        out_specs=pl.BlockSpec((1, n_top, heads, LANES), lambda b: (b, 0, 0, 0)),
        out_shape=jax.ShapeDtypeStruct((n_seq, n_top, heads, LANES), jnp.int32),
        compiler_params=_params(planned, 1),
        name="moba_select",
    )(means, q)
    return idx[..., 0]


def _moba_sample_attend(k_refs, v_refs, q, k_new, v_new, head, scale):
    page, group, hd = k_refs[0].shape[1:]
    rows = page * group
    mine = lax.broadcasted_iota(jnp.int32, (SUBLANES, rows), 1) % group == head % group
    q8 = jnp.broadcast_to(q * scale, (SUBLANES, hd))
    q8b = q8.astype(BF16)
    scores = [jnp.where(mine, _nt(q8b, k_ref[0].reshape(rows, hd).astype(BF16)), NEG_BIG)
              for k_ref in k_refs]
    s_own = jnp.sum(q8 * k_new, axis=-1, keepdims=True)
    mx = s_own
    for s in scores:
        mx = jnp.maximum(mx, s.max(axis=-1, keepdims=True))
    p_own = jnp.exp(s_own - mx)
    denom = p_own
    acc = p_own * v_new
    for s, v_ref in zip(scores, v_refs):
        p = jnp.exp(s - mx)
        denom = denom + jnp.sum(p, axis=-1, keepdims=True)
        acc = acc + jnp.dot(p.astype(BF16), v_ref[0].reshape(rows, hd).astype(BF16),
                            preferred_element_type=F32)
    return (acc / denom)[0:1]


def _moba_kernel(pt_ref, idx_ref, q_ref, k_ref, v_ref, qs_ref, ks_ref, vs_ref, *refs,
                 n_blocks, scale, n_sel, pairs_per_step, n_groups, heads):
    n_slabs = 2 * n_sel * pairs_per_step
    slab_refs = refs[:n_slabs]
    o_ref, os_ref, kb_ref, vt_ref, mean_ref, p_ref = refs[n_slabs:]
    _moba_prompt_attend(q_ref, k_ref, v_ref, o_ref, kb_ref, vt_ref, mean_ref, p_ref,
                        n_blocks=n_blocks, scale=scale)
    group = jnp.minimum(pl.program_id(0) * pl.num_programs(1) + pl.program_id(1), n_groups - 1)
    for t in range(pairs_per_step):
        pair_refs = slab_refs[2 * n_sel * t:2 * n_sel * (t + 1)]
        head = (group * pairs_per_step + t) % heads
        out = _moba_sample_attend(pair_refs[:n_sel], pair_refs[n_sel:], qs_ref[0, t:t + 1, :],
                                  ks_ref[0, t:t + 1, :], vs_ref[0, t:t + 1, :], head, scale)
        os_ref[0, t:t + 1, :] = out.astype(os_ref.dtype)


def _moba(q, k, v, q_s, k_s, v_s, page_table, picks, cache_k, cache_v, batch, seq):
    n_seq, n_pages = page_table.shape
    _, page, heads, hd = cache_k.shape
    assert seq % MOBA_BLOCK == 0 and hd == LANES and q.shape[1] == heads * hd
    n_blocks = seq // MOBA_BLOCK
    mean_rows = -(-n_blocks // SUBLANES) * SUBLANES
    pages_per_block = MOBA_BLOCK // page
    n_top = picks.shape[1]
    n_sel = n_top * pages_per_block
    n_steps = batch * heads
    n_pairs = n_seq * heads
    pairs_per_step = -(-n_pairs // n_steps)
    assert n_pairs % pairs_per_step == 0
    n_groups = n_pairs // pairs_per_step

    group = lambda b, h: jnp.minimum(b * heads + h, n_groups - 1)

    def slab_spec(t, sel, pp):
        def index(b, h, pt, ix):
            pair = group(b, h) * pairs_per_step + t
            s, hs = pair // heads, pair % heads
            return (pt[s, ix[s, sel, hs] * pages_per_block + pp], 0, hs // SUBLANES, 0)
        return pl.BlockSpec((1, page, SUBLANES, hd), index)

    pair_slabs = lambda t: [slab_spec(t, sel, pp) for sel in range(n_top) for pp in range(pages_per_block)]
    slab_specs = [spec for t in range(pairs_per_step) for spec in pair_slabs(t) + pair_slabs(t)]
    slab_args = [c for t in range(pairs_per_step) for c in [cache_k] * n_sel + [cache_v] * n_sel]
    head_rows = pl.BlockSpec((seq, hd), lambda b, h, pt, ix: (b, h))
    pair_rows = pl.BlockSpec((1, pairs_per_step, hd), lambda b, h, pt, ix: (group(b, h), 0, 0))
    as_pairs = lambda x: x.reshape(n_groups, pairs_per_step, hd)
    planned = (2 * 3 * seq * hd * 4 + 2 * seq * hd * 2 + seq * hd * 2 + 2 * hd * seq * 2
               + seq * MOBA_BLOCK * 2 + 4 * seq * MOBA_BLOCK * 4
               + 2 * len(slab_specs) * page * SUBLANES * hd * 4 + 16 * page * SUBLANES * hd * 4)
    grid_spec = pltpu.PrefetchScalarGridSpec(
        num_scalar_prefetch=2,
        grid=(batch, heads),
        in_specs=[head_rows, head_rows, head_rows, pair_rows, pair_rows, pair_rows] + slab_specs,
        out_specs=[head_rows, pair_rows],
        scratch_shapes=[pltpu.VMEM((seq, hd), BF16), pltpu.VMEM((2 * hd, seq), BF16),
                        pltpu.VMEM((mean_rows, hd), F32), pltpu.VMEM((seq, MOBA_BLOCK), BF16)],
    )
    att, att_s = pl.pallas_call(
        functools.partial(_moba_kernel, n_blocks=n_blocks, scale=hd ** -0.5, n_sel=n_sel,
                          pairs_per_step=pairs_per_step, n_groups=n_groups, heads=heads),
        grid_spec=grid_spec,
        out_shape=[jax.ShapeDtypeStruct((batch * seq, heads * hd), BF16),
                   jax.ShapeDtypeStruct((n_groups, pairs_per_step, hd), BF16)],
        compiler_params=_params(planned, 2),
        name="moba",
    )(page_table, picks, q, k, v, as_pairs(q_s), as_pairs(k_s), as_pairs(v_s), *slab_args)
    return att, att_s.reshape(n_seq, heads * hd)


def _rope_tables(pos, hd):
    half = hd // 2
    inv = ROPE_THETA ** (-jnp.arange(half, dtype=F32) / half)
    ang = pos[:, None] * inv[None, :]
    cos, sin = jnp.cos(ang), jnp.sin(ang)
    return jnp.concatenate([cos, cos], axis=-1), jnp.concatenate([-sin, sin], axis=-1)


def kernel(x_prompt, x_sample, cache_k, cache_v, page_table, state_gla, state_conv, g_mix, w_in_a,
           w_a2, b_a2, g_gla_head, w_out_a, g_kv, w_kv, w_q, w_out_b, g_ffn, w_up, w_conv, b_conv,
           w_down, g_final):
    n_b, seq, d = x_prompt.shape
    n_s, t_s, _ = x_sample.shape
    assert t_s == 1
    _, _, gla_h, dk, dv = state_gla.shape
    _, page, att_h, hd = cache_k.shape
    n_pages = page_table.shape[1]
    past_len = n_pages * page
    rank = w_a2.shape[1]
    qk_cols, v_cols = gla_h * dk, gla_h * dv
    proj_cols = 2 * qk_cols + 2 * v_cols
    da = att_h * hd
    assert rank <= LANES and len(w_in_a) == 1 and len(w_q) == 1

    xp = x_prompt.reshape(n_b * seq, d)
    xs = x_sample.reshape(n_s, d)
    rope_p = _rope_tables(jnp.arange(seq, dtype=F32), hd)
    rope_s = tuple(jnp.tile(t, (n_s, 1)) for t in _rope_tables(past_len + jnp.arange(t_s, dtype=F32), hd))

    w_a1_pad = jnp.pad(w_in_a[:, :, proj_cols:], ((0, 0), (0, 0), (0, LANES - rank)))
    w_a2_pad = jnp.pad(w_a2[0], ((0, LANES - rank), (0, 0)))
    conv_rows = state_conv.transpose(0, 2, 1, 3)
    b_conv3 = b_conv[:, None, :]
    down_split = 2 if (w_down.shape[1] // 2) % LANES == 0 else 1

    def norm(h, gains, dtype=BF16):
        return tuple(zip(_rmsnorm(h[0], gains, dtype), _rmsnorm(h[1], gains, dtype)))

    def layer0(h):
        (hn,) = norm(h, g_mix[0:1])
        proj_p, proj_s = _matmul(*hn, w_in_a, 0, proj_cols)
        a_p, a_s = _matmul(*hn, w_a1_pad, 0, LANES)
        gate_p = _gla_gate(a_p, w_a2_pad, b_a2[0:1])
        gate_s = _gla_gate(a_s, w_a2_pad, b_a2[0:1])
        og_p, s_p, key_means = _gla_prompt(proj_p, gate_p, g_gla_head[0:1], page_table, cache_k,
                                           n_b, seq, gla_h, dk, dv)
        og_s, s_s = _gla_sample(proj_s.reshape(n_s, 1, proj_cols), gate_s.reshape(n_s, 1, qk_cols),
                                g_gla_head[0:1], state_gla, 0, gla_h, dk, dv)
        return _matmul(og_p, og_s.reshape(n_s, v_cols), w_out_a, 0, d, res=h), (s_p, s_s), key_means

    def conv_ffn(h, layer):
        (hn,) = norm(h, g_ffn[layer:layer + 1])
        act_p, st_p, act_s, u_s = _up(*hn, w_up, w_conv, b_conv3, conv_rows, layer, n_b, seq)
        st_s = jnp.stack([state_conv[layer][:, 1, :], u_s], axis=1)
        return _matmul(act_p, act_s, w_down, layer, d, k_split=down_split, res=h), (st_p, st_s)

    def layer1(h, key_means):
        hn, kvn = norm(h, jnp.stack([g_mix[1], g_kv]))
        rope = (rope_p, rope_s)
        k = _matmul(*kvn, w_kv[None], 0, da, rope=rope)
        v = _matmul(*kvn, w_kv[None], 0, da, w_col0=da)
        q = _matmul(*hn, w_q, 0, da, rope=rope)
        picks = _moba_select(key_means, q[1].reshape(n_s, att_h, hd))
        att = _moba(q[0], k[0], v[0], q[1], k[1], v[1], page_table, picks, cache_k, cache_v, n_b, seq)
        return _matmul(*att, w_out_b, 0, d, res=h), k, v

    h, (gla_p, gla_s), key_means = layer0((xp, xs))
    h, conv0 = conv_ffn(h, 0)
    h, (k_p, k_s), (v_p, v_s) = layer1(h, key_means)
    h, conv1 = conv_ffn(h, 1)
    ((y_p, y_s),) = norm(h, g_final[None, :], F32)
    gla_p, gla_s = gla_p[None], gla_s[None]
    conv_p, conv_s = jnp.stack([conv0[0], conv1[0]]), jnp.stack([conv0[1], conv1[1]])
    return (y_p.reshape(n_b, seq, d), y_s.reshape(n_s, t_s, d),
            k_p.reshape(n_b, seq, att_h, hd), v_p.reshape(n_b, seq, att_h, hd), gla_p, conv_p,
            k_s.reshape(n_s, t_s, att_h, hd), v_s.reshape(n_s, t_s, att_h, hd), gla_s, conv_s)
```
